```python
import jax, jax.numpy as jnp
from jax import lax
import numpy as np

D_MODEL = 1024
BATCH = 2
SEQ = 8192
DEPTH = 1

POOL_WIDTH = D_MODEL
POOL_WINDOWS = (2, 4, 8, 16)
N_POOL_GROUPS = len(POOL_WINDOWS)
POOL_GROUP_DIM = POOL_WIDTH // N_POOL_GROUPS
MLSTM_WIDTH = D_MODEL
N_HEADS = 4
HEAD_DIM = MLSTM_WIDTH // N_HEADS
QKV_BLOCK = 4
N_QKV_BLOCKS = MLSTM_WIDTH // QKV_BLOCK
CONV_K = 4
CHUNK = 128
MIX_WIDTH = POOL_WIDTH + MLSTM_WIDTH
IN_WIDTH = 2 * POOL_WIDTH + 3 * MLSTM_WIDTH
EPS = 1e-6

kernel_name = "hybrid_pool_mlstm_parallel_heads"


def rmsnorm(x, w):
    xf = x.astype(jnp.float32)
    y = xf * lax.rsqrt(jnp.mean(xf * xf, axis=-1, keepdims=True) + EPS)
    return (y * w.astype(jnp.float32)).astype(x.dtype)


def pool_mixer(u, pool_w, pool_scale):
    B, S, _ = u.shape
    uf = u.astype(jnp.float32)
    cs = jnp.cumsum(uf, axis=1)
    pos = jnp.arange(1, S + 1, dtype=jnp.float32)
    outs = []
    for g, win in enumerate(POOL_WINDOWS):
        sl = slice(g * POOL_GROUP_DIM, (g + 1) * POOL_GROUP_DIM)
        seg = cs[..., sl]
        prev = jnp.pad(seg, ((0, 0), (win, 0), (0, 0)))[:, :S]
        cnt = jnp.minimum(pos, float(win))[None, :, None]
        outs.append((seg - prev) / cnt - uf[..., sl])
    d = jnp.stack(outs, axis=2).astype(u.dtype)
    y = jnp.einsum('bsgc,gcd->bsgd', d, pool_w).reshape(B, S, POOL_WIDTH)
    return y * pool_scale


def causal_conv(u, w, b):
    S = u.shape[1]
    up = jnp.pad(u, ((0, 0), (CONV_K - 1, 0), (0, 0)))
    y = sum(up[:, j:j + S] * w[j] for j in range(CONV_K))
    return y + b


def headwise(u, w):
    B, S, C = u.shape
    return jnp.einsum('bsnc,ncd->bsnd', u.reshape(B, S, N_QKV_BLOCKS, QKV_BLOCK), w).reshape(B, S, C)


def mlstm_chunkwise(q, k, v, i_pre, f_pre):
    B, H, S, DH = q.shape
    NC = S // CHUNK
    q = q.reshape(B, H, NC, CHUNK, DH)
    k = k.reshape(B, H, NC, CHUNK, DH) * (DH ** -0.5)
    v = v.reshape(B, H, NC, CHUNK, DH)
    log_f = jax.nn.log_sigmoid(f_pre).reshape(B, H, NC, CHUNK)
    ig = i_pre.reshape(B, H, NC, CHUNK)
    b = jnp.cumsum(log_f, axis=-1)
    g = b[..., -1]
    a = g[..., None] - b + ig
    a_max = jnp.max(a, axis=-1)

    def step(carry, xs):
        C, n, m = carry
        k_c, v_c, a_c, g_c, amax_c = xs
        m_new = jnp.maximum(g_c + m, amax_c)
        decay = jnp.exp(g_c + m - m_new)
        w = jnp.exp(a_c - m_new[..., None])
        C_new = decay[..., None, None] * C + jnp.einsum('bhl,bhle,bhld->bhed', w, v_c, k_c)
        n_new = decay[..., None] * n + jnp.einsum('bhl,bhld->bhd', w, k_c)
        return (C_new, n_new, m_new), (C, n, m)

    init = (jnp.zeros((B, H, DH, DH), jnp.float32),
            jnp.zeros((B, H, DH), jnp.float32),
            jnp.zeros((B, H), jnp.float32))
    xs = (jnp.moveaxis(k, 2, 0), jnp.moveaxis(v, 2, 0), jnp.moveaxis(a, 2, 0),
          jnp.moveaxis(g, 2, 0), jnp.moveaxis(a_max, 2, 0))
    _, (C_s, n_s, m_s) = lax.scan(step, init, xs)
    C_s = jnp.moveaxis(C_s, 0, 2)
    n_s = jnp.moveaxis(n_s, 0, 2)
    m_s = jnp.moveaxis(m_s, 0, 2)

    mask = jnp.tril(jnp.ones((CHUNK, CHUNK), dtype=bool))
    d_log = b[..., :, None] - b[..., None, :] + ig[..., None, :]
    d_log = jnp.where(mask, d_log, -jnp.inf)
    m_inter = b + m_s[..., None]
    m_t = jnp.maximum(jnp.max(d_log, axis=-1), m_inter)
    W = jnp.exp(d_log - m_t[..., None]) * jnp.einsum('bhcld,bhcsd->bhcls', q, k)
    inter = jnp.exp(m_inter - m_t)
    num = jnp.einsum('bhcls,bhcse->bhcle', W, v) + inter[..., None] * jnp.einsum('bhced,bhcld->bhcle', C_s, q)
    den = jnp.sum(W, axis=-1) + inter * jnp.einsum('bhcd,bhcld->bhcl', n_s, q)
    h = num / jnp.maximum(jnp.abs(den), jnp.exp(-m_t))[..., None]
    return h.reshape(B, H, S, DH)


def mlstm_branch(mx, mz, mo, conv_w, conv_b, w_q, w_k, w_v, w_if, b_if, mh_norm_w, m_skip):
    B, S, _ = mx.shape
    xc = jax.nn.silu(causal_conv(mx, conv_w, conv_b))
    q = headwise(xc, w_q)
    k = headwise(xc, w_k)
    v = headwise(mx, w_v)
    gates = (jnp.concatenate([q, k, v], axis=-1) @ w_if + b_if).astype(jnp.float32)
    i_pre = jnp.transpose(gates[..., :N_HEADS], (0, 2, 1))
    f_pre = jnp.transpose(gates[..., N_HEADS:], (0, 2, 1))

    def to_heads(t):
        return jnp.transpose(t.astype(jnp.float32).reshape(B, S, N_HEADS, HEAD_DIM), (0, 2, 1, 3))

    h = mlstm_chunkwise(to_heads(q), to_heads(k), to_heads(v), i_pre, f_pre)
    h = jnp.transpose(h, (0, 2, 1, 3))
    h = jax.nn.sigmoid(mo.astype(jnp.float32)).reshape(B, S, N_HEADS, HEAD_DIM) * h
    mu = jnp.mean(h, axis=-1, keepdims=True)
    var = jnp.mean(jnp.square(h - mu), axis=-1, keepdims=True)
    hn = ((h - mu) * lax.rsqrt(var + EPS)).reshape(B, S, MLSTM_WIDTH) * mh_norm_w.astype(jnp.float32)
    out = (hn.astype(mx.dtype) + m_skip * xc) * jax.nn.silu(mz)
    return out


def setup_inputs(seed: int = 0) -> dict:
    key = jax.random.key(seed)
    ks = jax.random.split(key, 20)
    f32 = jnp.float32
    nrm = lambda k, shape, s: jax.random.normal(k, shape, f32) * s
    x = jax.random.normal(ks[0], (BATCH, SEQ, D_MODEL), f32)
    norm_w = 1.0 + nrm(ks[1], (DEPTH, D_MODEL), 0.02)
    w_in = nrm(ks[2], (DEPTH, D_MODEL, IN_WIDTH), D_MODEL ** -0.5)
    pool_w = nrm(ks[3], (DEPTH, N_POOL_GROUPS, POOL_GROUP_DIM, POOL_GROUP_DIM), POOL_GROUP_DIM ** -0.5)
    pool_scale = 1.0 + nrm(ks[4], (DEPTH, POOL_WIDTH), 0.02)
    conv_w = nrm(ks[5], (DEPTH, CONV_K, MLSTM_WIDTH), CONV_K ** -0.5)
    conv_b = nrm(ks[6], (DEPTH, MLSTM_WIDTH), 0.02)
    w_q = nrm(ks[7], (DEPTH, N_QKV_BLOCKS, QKV_BLOCK, QKV_BLOCK), QKV_BLOCK ** -0.5)
    w_k = nrm(ks[8], (DEPTH, N_QKV_BLOCKS, QKV_BLOCK, QKV_BLOCK), QKV_BLOCK ** -0.5)
    w_v = nrm(ks[9], (DEPTH, N_QKV_BLOCKS, QKV_BLOCK, QKV_BLOCK), QKV_BLOCK ** -0.5)
    w_if = nrm(ks[10], (DEPTH, 3 * MLSTM_WIDTH, 2 * N_HEADS), 0.1 * (3 * MLSTM_WIDTH) ** -0.5)
    i_bias = nrm(ks[11], (DEPTH, N_HEADS), 0.1)
    f_bias = jnp.broadcast_to(jnp.linspace(3.0, 6.0, N_HEADS, dtype=f32), (DEPTH, N_HEADS)) + nrm(ks[12], (DEPTH, N_HEADS), 0.1)
    b_if = jnp.concatenate([i_bias, f_bias], axis=-1)
    mh_norm_w = 1.0 + nrm(ks[13], (DEPTH, MLSTM_WIDTH), 0.02)
    m_skip = 1.0 + nrm(ks[14], (DEPTH, MLSTM_WIDTH), 0.02)
    w_out = nrm(ks[15], (DEPTH, MIX_WIDTH, D_MODEL), MIX_WIDTH ** -0.5)
    final_norm_w = 1.0 + nrm(ks[16], (D_MODEL,), 0.02)
    return {"x": x, "norm_w": norm_w, "w_in": w_in, "pool_w": pool_w, "pool_scale": pool_scale,
            "conv_w": conv_w, "conv_b": conv_b, "w_q": w_q, "w_k": w_k, "w_v": w_v,
            "w_if": w_if, "b_if": b_if, "mh_norm_w": mh_norm_w, "m_skip": m_skip,
            "w_out": w_out, "final_norm_w": final_norm_w}


def reference(x, norm_w, w_in, pool_w, pool_scale, conv_w, conv_b, w_q, w_k, w_v,
              w_if, b_if, mh_norm_w, m_skip, w_out, final_norm_w):
    P, M = POOL_WIDTH, MLSTM_WIDTH
    for l in range(DEPTH):
        u = rmsnorm(x, norm_w[l])
        proj = u @ w_in[l]
        pool_x = proj[..., :P]
        pool_z = proj[..., P:2 * P]
        m_x = proj[..., 2 * P:2 * P + M]
        m_z = proj[..., 2 * P + M:2 * P + 2 * M]
        m_o = proj[..., 2 * P + 2 * M:]
        y_pool = pool_mixer(pool_x, pool_w[l], pool_scale[l]) * jax.nn.silu(pool_z)
        y_mlstm = mlstm_branch(m_x, m_z, m_o, conv_w[l], conv_b[l], w_q[l], w_k[l], w_v[l],
                               w_if[l], b_if[l], mh_norm_w[l], m_skip[l])
        mixed = jnp.concatenate([y_pool, y_mlstm], axis=-1)
        x = x + mixed @ w_out[l]
    return rmsnorm(x, final_norm_w)
```

```python
import functools

import jax
import jax.numpy as jnp
from jax import lax
from jax.experimental import pallas as pl
from jax.experimental.pallas import tpu as pltpu

D_MODEL = 1024
POOL_WINDOWS = (2, 4, 8, 16)
N_GROUPS = 4
GROUP_DIM = 256
QKV_BLOCK = 4
CONV_K = 4
CHUNK = 128
EPS = 1e-6

TILE_TOKENS = 256
POOL_HALO = 16
CONV_HALO = 8
GATE_LANES = 128
F_GATE_ROW = 8
VMEM_LIMIT_BYTES = 58 * 1024 * 1024

_NT = (((1,), (1,)), ((), ()))
_TN = (((0,), (0,)), ((), ()))


def _dot(a, b):
    return jnp.dot(a, b, preferred_element_type=jnp.float32)


def _lane_scan(v, lane, op, identity):
    for s in (1, 2, 4, 8, 16, 32, 64):
        v = op(v, jnp.where(lane >= s, pltpu.roll(v, s, axis=1), identity))
    return v


def _fused_kernel(x_ref, normw_ref, win_ref, poolw_ref, pscale_ref, convw_ref, convb_ref,
                  wqk_ref, wv_ref, wif_ref, bif_ref, mhw_ref, mskip_ref, wout_ref, fnw_ref,
                  o_ref,
                  px_ref, mxe_ref, xc_ref, qkv_ref, hs_ref, mixed_ref, ct_ref, n_ref, m_ref):
    tm = TILE_TOKENS
    f32, bf16 = jnp.float32, jnp.bfloat16
    j = pl.program_id(1)

    @pl.when(j == 0)
    def _():
        px_ref[0:POOL_HALO, :] = jnp.zeros((POOL_HALO, D_MODEL), f32)
        mxe_ref[0:CONV_HALO, :] = jnp.zeros((CONV_HALO, D_MODEL), f32)
        ct_ref[...] = jnp.zeros(ct_ref.shape, f32)
        n_ref[...] = jnp.zeros(n_ref.shape, f32)
        m_ref[...] = jnp.zeros(m_ref.shape, f32)

    x = x_ref[...]
    u = x * lax.rsqrt(jnp.mean(x * x, axis=-1, keepdims=True) + EPS)
    u = (u * normw_ref[...]).astype(bf16)

    def proj(k):
        return _dot(u, win_ref[:, k * D_MODEL:(k + 1) * D_MODEL])

    px_ref[POOL_HALO:POOL_HALO + tm, :] = proj(0)
    pos = j * tm + lax.broadcasted_iota(jnp.int32, (tm, 128), 0)
    pz = proj(1)
    for g, win in enumerate(POOL_WINDOWS):
        cols = slice(g * GROUP_DIM, (g + 1) * GROUP_DIM)
        a = px_ref[:, cols]
        wsum, span = a, 1
        while span < win:
            wsum = wsum + pltpu.roll(wsum, span, axis=0)
            span *= 2
        inv = 1.0 / jnp.minimum(pos + 1, win).astype(f32)
        inv = jnp.concatenate([inv, inv], axis=1)
        d = wsum[POOL_HALO:, :] * inv - a[POOL_HALO:, :]
        y = _dot(d.astype(bf16), poolw_ref[g])
        zg = pz[:, cols]
        y = y * pscale_ref[:, cols] * (zg * jax.nn.sigmoid(zg))
        mixed_ref[:, cols] = y.astype(bf16)
    px_ref[0:POOL_HALO, :] = px_ref[tm:tm + POOL_HALO, :]

    mxe_ref[CONV_HALO:CONV_HALO + tm, :] = proj(2)
    me = mxe_ref[...]
    conv = convb_ref[...] + convw_ref[CONV_K - 1:CONV_K, :] * me[CONV_HALO:, :]
    for back in range(1, CONV_K):
        tap = CONV_K - 1 - back
        conv = conv + convw_ref[tap:tap + 1, :] * pltpu.roll(me, back, axis=0)[CONV_HALO:, :]
    xc = conv * jax.nn.sigmoid(conv)
    xc_ref[...] = xc
    mxe_ref[0:CONV_HALO, :] = mxe_ref[tm:tm + CONV_HALO, :]

    xcb = xc.astype(bf16)
    mxb = me[CONV_HALO:, :].astype(bf16)
    for g in range(N_GROUPS):
        cols = slice(g * GROUP_DIM, (g + 1) * GROUP_DIM)
        qk = _dot(xcb[:, cols], wqk_ref[g])
        v = _dot(mxb[:, cols], wv_ref[g])
        qkv_ref[:, g * GROUP_DIM:(g + 1) * GROUP_DIM] = qk[:, :GROUP_DIM].astype(bf16)
        qkv_ref[:, D_MODEL + g * GROUP_DIM:D_MODEL + (g + 1) * GROUP_DIM] = qk[:, GROUP_DIM:].astype(bf16)
        qkv_ref[:, 2 * D_MODEL + g * GROUP_DIM:2 * D_MODEL + (g + 1) * GROUP_DIM] = v.astype(bf16)

    gates = _dot(qkv_ref[...], wif_ref[...]) + bif_ref[...]
    gates_t = gates.T
    i_rows = gates_t[0:8, :]
    f_rows = gates_t[F_GATE_ROW:F_GATE_ROW + 8, :]

    lane = lax.broadcasted_iota(jnp.int32, (8, CHUNK), 1)
    tri = (lax.broadcasted_iota(jnp.int32, (CHUNK, CHUNK), 0)
           >= lax.broadcasted_iota(jnp.int32, (CHUNK, CHUNK), 1))

    for c in range(tm // CHUNK):
        rows = slice(c * CHUNK, (c + 1) * CHUNK)
        ig = i_rows[:, rows]
        fp = f_rows[:, rows]
        log_f = jnp.minimum(fp, 0.0) - jnp.log(1.0 + jnp.exp(-jnp.abs(fp)))
        b = _lane_scan(log_f, lane, jnp.add, 0.0)
        g_tot = b[:, CHUNK - 1:CHUNK]
        m_old = m_ref[...]
        ub = ig - b
        cmax = _lane_scan(ub, lane, jnp.maximum, -jnp.inf)
        stab = jnp.maximum(cmax, m_old)
        m_new = jnp.maximum(g_tot + m_old, g_tot + cmax[:, CHUNK - 1:CHUNK])
        decay = jnp.exp(g_tot + m_old - m_new)
        w_state = jnp.exp(g_tot + ub - m_new)
        inter = jnp.exp(m_old - stab)
        exp_neg_mt = jnp.exp(-(b + stab))
        m_ref[...] = m_new
        pack = jnp.concatenate(
            [stab, inter, exp_neg_mt, w_state, jnp.zeros((CHUNK - 32, CHUNK), f32)], axis=0)
        pack_t = pack.T

        for h in range(N_GROUPS):
            cols = slice(h * GROUP_DIM, (h + 1) * GROUP_DIM)
            q = qkv_ref[rows, h * GROUP_DIM:(h + 1) * GROUP_DIM]
            k = qkv_ref[rows, D_MODEL + h * GROUP_DIM:D_MODEL + (h + 1) * GROUP_DIM]
            v = qkv_ref[rows, 2 * D_MODEL + h * GROUP_DIM:2 * D_MODEL + (h + 1) * GROUP_DIM]
            stab_c = pack_t[:, h:h + 1]
            inter_c = pack_t[:, 8 + h:9 + h]
            emt_c = pack_t[:, 16 + h:17 + h]
            w_c = pack_t[:, 24 + h:25 + h]

            s = lax.dot_general(q, k, _NT, preferred_element_type=f32)
            e = jnp.where(tri, ub[h:h + 1, :] - stab_c, -jnp.inf)
            wt = jnp.exp(e) * s
            ct = ct_ref[h]
            n_row = n_ref[h:h + 1, :]
            qf = q.astype(f32)
            num = _dot(wt.astype(bf16), v) + inter_c * _dot(q, ct.astype(bf16))
            den = (jnp.sum(wt, axis=-1, keepdims=True)
                   + inter_c * jnp.sum(qf * n_row, axis=-1, keepdims=True))
            hs_ref[rows, cols] = num * (1.0 / jnp.maximum(jnp.abs(den), emt_c))

            wk = w_c * k.astype(f32)
            dec = decay[h:h + 1, 0:1]
            ct_ref[h] = dec * ct + lax.dot_general(wk.astype(bf16), v, _TN,
                                                   preferred_element_type=f32)
            n_ref[h:h + 1, :] = dec * n_row + jnp.sum(wk, axis=0, keepdims=True)

    mz = proj(3)
    mo = proj(4)
    for h in range(N_GROUPS):
        cols = slice(h * GROUP_DIM, (h + 1) * GROUP_DIM)
        hg = jax.nn.sigmoid(mo[:, cols]) * hs_ref[:, cols]
        mu = jnp.mean(hg, axis=-1, keepdims=True)
        cen = hg - mu
        var = jnp.mean(cen * cen, axis=-1, keepdims=True)
        hn = cen * lax.rsqrt(var + EPS) * mhw_ref[:, cols]
        zg = mz[:, cols]
        out = (hn + mskip_ref[:, cols] * xc_ref[:, cols]) * (zg * jax.nn.sigmoid(zg))
        mixed_ref[:, D_MODEL + h * GROUP_DIM:D_MODEL + (h + 1) * GROUP_DIM] = out.astype(bf16)

    y = x + _dot(mixed_ref[...], wout_ref[...])
    y = y * lax.rsqrt(jnp.mean(y * y, axis=-1, keepdims=True) + EPS)
    o_ref[...] = y * fnw_ref[...]


def _block_diag_groups(w):
    nb = GROUP_DIM // QKV_BLOCK
    wg = w.reshape(N_GROUPS, nb, QKV_BLOCK, QKV_BLOCK)
    eye = jnp.eye(nb, dtype=w.dtype)
    dense = wg[:, :, :, None, :] * eye[None, :, None, :, None]
    return dense.reshape(N_GROUPS, GROUP_DIM, GROUP_DIM)


def _const_spec(shape):
    zeros = (0,) * len(shape)
    return pl.BlockSpec(shape, lambda b, j: zeros, pipeline_mode=pl.Buffered(1))


@jax.jit
def kernel(x, norm_w, w_in, pool_w, pool_scale, conv_w, conv_b, w_q, w_k, w_v, w_if, b_if,
           mh_norm_w, m_skip, w_out, final_norm_w):
    batch, seq, d = x.shape
    assert d == D_MODEL and norm_w.shape[0] == 1 and seq % TILE_TOKENS == 0
    tm = TILE_TOKENS
    n_tiles = seq // tm
    bf16, f32 = jnp.bfloat16, jnp.float32

    k_scale = GROUP_DIM ** -0.5
    wqk = jnp.concatenate([_block_diag_groups(w_q[0]),
                           _block_diag_groups(w_k[0]) * k_scale], axis=2).astype(bf16)
    wv = _block_diag_groups(w_v[0]).astype(bf16)
    wif_rows = w_if[0] * jnp.concatenate(
        [jnp.ones((D_MODEL, 1), f32), jnp.full((D_MODEL, 1), 1.0 / k_scale, f32),
         jnp.ones((D_MODEL, 1), f32)], axis=0)
    wif = jnp.zeros((3 * D_MODEL, GATE_LANES), f32)
    wif = wif.at[:, 0:N_GROUPS].set(wif_rows[:, :N_GROUPS])
    wif = wif.at[:, F_GATE_ROW:F_GATE_ROW + N_GROUPS].set(wif_rows[:, N_GROUPS:]).astype(bf16)
    bif = jnp.zeros((1, GATE_LANES), f32)
    bif = bif.at[0, 0:N_GROUPS].set(b_if[0, :N_GROUPS])
    bif = bif.at[0, F_GATE_ROW:F_GATE_ROW + N_GROUPS].set(b_if[0, N_GROUPS:])

    row = lambda a: a.reshape(1, -1).astype(f32)
    operands = (
        x.reshape(batch * seq, d), row(norm_w[0]), w_in[0].astype(bf16), pool_w[0].astype(bf16),
        row(pool_scale[0]), conv_w[0], row(conv_b[0]), wqk, wv, wif, bif,
        row(mh_norm_w[0]), row(m_skip[0]), w_out[0].astype(bf16), row(final_norm_w))

    tile_spec = pl.BlockSpec((tm, d), lambda b, j: (b * n_tiles + j, 0))
    in_specs = [tile_spec] + [_const_spec(a.shape) for a in operands[1:]]

    out = pl.pallas_call(
        _fused_kernel,
        grid=(batch, n_tiles),
        in_specs=in_specs,
        out_specs=tile_spec,
        out_shape=jax.ShapeDtypeStruct((batch * seq, d), f32),
        scratch_shapes=[
            pltpu.VMEM((tm + POOL_HALO, d), f32),
            pltpu.VMEM((tm + CONV_HALO, d), f32),
            pltpu.VMEM((tm, d), f32),
            pltpu.VMEM((tm, 3 * d), bf16),
            pltpu.VMEM((tm, d), f32),
            pltpu.VMEM((tm, 2 * d), bf16),
            pltpu.VMEM((N_GROUPS, GROUP_DIM, GROUP_DIM), f32),
            pltpu.VMEM((8, GROUP_DIM), f32),
            pltpu.VMEM((8, CHUNK), f32),
        ],
        compiler_params=pltpu.CompilerParams(
            dimension_semantics=("arbitrary", "arbitrary"),
            vmem_limit_bytes=VMEM_LIMIT_BYTES),
        name="hybrid_pool_mlstm_block",
    )(*operands)
    return out.reshape(batch, seq, d)
```

```python
import jax
import jax.numpy as jnp
from jax import lax
from jax.experimental import pallas as pl
from jax.experimental.pallas import tpu as pltpu

D_MODEL = 1024
POOL_WINDOWS = (2, 4, 8, 16)
N_GROUPS = 4
GROUP_DIM = 256
QKV_BLOCK = 4
CONV_K = 4
CHUNK = 128
EPS = 1e-6

TILE_TOKENS = 256
POOL_HALO = 16
CONV_HALO = 8
GATE_LANES = 128
GATE_ROWS = 16
F_GATE_ROW = 8
VMEM_LIMIT_BYTES = 58 * 1024 * 1024

_NT = (((1,), (1,)), ((), ()))


def _dot(a, b):
    return jnp.dot(a, b, preferred_element_type=jnp.float32)


def _lane_scan(v, lane, op, identity):
    for s in (1, 2, 4, 8, 16, 32, 64):
        v = op(v, jnp.where(lane >= s, pltpu.roll(v, s, axis=1), identity))
    return v


def _fused_kernel(x_ref, normw_ref, win_ref, poolw_ref, pscale_ref, convw_ref, convb_ref,
                  wq_ref, wkt_ref, wv_ref, wifq_ref, wifkt_ref, wifv_ref, bif_ref,
                  mhw_ref, mskip_ref, wout_ref, fnw_ref,
                  o_ref,
                  px_ref, mxe_ref, xc_ref, q_ref, kt_ref, v_ref, hs_ref, mixed_ref,
                  ct_ref, n_ref, m_ref):
    tm = TILE_TOKENS
    f32, bf16 = jnp.float32, jnp.bfloat16
    j = pl.program_id(1)

    @pl.when(j == 0)
    def _():
        px_ref[0:POOL_HALO, :] = jnp.zeros((POOL_HALO, D_MODEL), f32)
        mxe_ref[0:CONV_HALO, :] = jnp.zeros((CONV_HALO, D_MODEL), f32)
        ct_ref[...] = jnp.zeros(ct_ref.shape, f32)
        n_ref[...] = jnp.zeros(n_ref.shape, f32)
        m_ref[...] = jnp.zeros(m_ref.shape, f32)

    x = x_ref[...]
    u = x * lax.rsqrt(jnp.mean(x * x, axis=-1, keepdims=True) + EPS)
    u = (u * normw_ref[...]).astype(bf16)

    def proj(k):
        return _dot(u, win_ref[:, k * D_MODEL:(k + 1) * D_MODEL])

    mxe_ref[CONV_HALO:CONV_HALO + tm, :] = proj(2)
    me = mxe_ref[...]
    conv = convb_ref[...] + convw_ref[CONV_K - 1:CONV_K, :] * me[CONV_HALO:, :]
    for back in range(1, CONV_K):
        tap = CONV_K - 1 - back
        conv = conv + convw_ref[tap:tap + 1, :] * pltpu.roll(me, back, axis=0)[CONV_HALO:, :]
    xc = conv * jax.nn.sigmoid(conv)
    xc_ref[...] = xc
    mxe_ref[0:CONV_HALO, :] = mxe_ref[tm:tm + CONV_HALO, :]

    xcb = xc.astype(bf16)
    mxb = me[CONV_HALO:, :].astype(bf16)
    for g in range(N_GROUPS):
        cols = slice(g * GROUP_DIM, (g + 1) * GROUP_DIM)
        q_ref[:, cols] = _dot(xcb[:, cols], wq_ref[g]).astype(bf16)
        kt_ref[cols, :] = lax.dot_general(wkt_ref[g], xcb[:, cols], _NT,
                                          preferred_element_type=f32).astype(bf16)
        v_ref[:, cols] = _dot(mxb[:, cols], wv_ref[g]).astype(bf16)

    gates_qv = _dot(q_ref[...], wifq_ref[...]) + _dot(v_ref[...], wifv_ref[...]) + bif_ref[...]
    gates_t = gates_qv.T[0:GATE_ROWS, :] + _dot(wifkt_ref[...], kt_ref[...])
    i_rows = gates_t[0:8, :]
    f_rows = gates_t[F_GATE_ROW:F_GATE_ROW + 8, :]

    lane = lax.broadcasted_iota(jnp.int32, (8, CHUNK), 1)
    tri = (lax.broadcasted_iota(jnp.int32, (CHUNK, CHUNK), 0)
           >= lax.broadcasted_iota(jnp.int32, (CHUNK, CHUNK), 1))

    scans = []
    for c in range(tm // CHUNK):
        rows = slice(c * CHUNK, (c + 1) * CHUNK)
        ig = i_rows[:, rows]
        fp = f_rows[:, rows]
        log_f = jnp.minimum(fp, 0.0) - jnp.log(1.0 + jnp.exp(-jnp.abs(fp)))
        b = _lane_scan(log_f, lane, jnp.add, 0.0)
        ub = ig - b
        cmax = _lane_scan(ub, lane, jnp.maximum, -jnp.inf)
        scans.append((b, ub, cmax))

    px_ref[POOL_HALO:POOL_HALO + tm, :] = proj(0)
    pos = j * tm + lax.broadcasted_iota(jnp.int32, (tm, 128), 0)
    pz = proj(1)
    for g, win in enumerate(POOL_WINDOWS):
        cols = slice(g * GROUP_DIM, (g + 1) * GROUP_DIM)
        a = px_ref[:, cols]
        wsum, span = a, 1
        while span < win:
            wsum = wsum + pltpu.roll(wsum, span, axis=0)
            span *= 2
        inv = 1.0 / jnp.minimum(pos + 1, win).astype(f32)
        inv = jnp.concatenate([inv, inv], axis=1)
        d = wsum[POOL_HALO:, :] * inv - a[POOL_HALO:, :]
        y = _dot(d.astype(bf16), poolw_ref[g])
        zg = pz[:, cols]
        y = y * pscale_ref[:, cols] * (zg * jax.nn.sigmoid(zg))
        mixed_ref[:, cols] = y.astype(bf16)
    px_ref[0:POOL_HALO, :] = px_ref[tm:tm + POOL_HALO, :]

    for c in range(tm // CHUNK):
        rows = slice(c * CHUNK, (c + 1) * CHUNK)
        b, ub, cmax = scans[c]
        g_tot = b[:, CHUNK - 1:CHUNK]
        m_old = m_ref[...]
        stab = jnp.maximum(cmax, m_old)
        m_new = jnp.maximum(g_tot + m_old, g_tot + cmax[:, CHUNK - 1:CHUNK])
        decay = jnp.exp(g_tot + m_old - m_new)
        w_state = jnp.exp(g_tot + ub - m_new)
        inter = jnp.exp(m_old - stab)
        exp_neg_mt = jnp.exp(-(b + stab))
        m_ref[...] = m_new
        pack = jnp.concatenate(
            [stab, inter, exp_neg_mt, jnp.zeros((CHUNK - 24, CHUNK), f32)], axis=0)
        pack_t = pack.T

        for h in range(N_GROUPS):
            cols = slice(h * GROUP_DIM, (h + 1) * GROUP_DIM)
            q = q_ref[rows, cols]
            kt = kt_ref[cols, rows]
            v = v_ref[rows, cols]
            stab_c = pack_t[:, h:h + 1]
            inter_c = pack_t[:, 8 + h:9 + h]
            emt_c = pack_t[:, 16 + h:17 + h]

            n_old = n_ref[h]
            s_ext = _dot(q, jnp.concatenate([kt, n_old.astype(bf16)], axis=1))
            e = jnp.where(tri, ub[h:h + 1, :] - stab_c, -jnp.inf)
            wt = jnp.exp(e) * s_ext[:, :CHUNK]
            ct = ct_ref[h]
            num = _dot(wt.astype(bf16), v) + inter_c * _dot(q, ct.astype(bf16))
            den = jnp.sum(wt, axis=-1, keepdims=True) + inter_c * s_ext[:, CHUNK:CHUNK + 1]
            hs_ref[rows, cols] = num * (1.0 / jnp.maximum(jnp.abs(den), emt_c))

            ktw = kt.astype(f32) * w_state[h:h + 1, :]
            dec = decay[h:h + 1, 0:1]
            ct_ref[h] = dec * ct + _dot(ktw.astype(bf16), v)
            n_ref[h] = dec * n_old + jnp.sum(ktw, axis=1, keepdims=True)

    mz = proj(3)
    mo = proj(4)
    for h in range(N_GROUPS):
        cols = slice(h * GROUP_DIM, (h + 1) * GROUP_DIM)
        hg = jax.nn.sigmoid(mo[:, cols]) * hs_ref[:, cols]
        mu = jnp.mean(hg, axis=-1, keepdims=True)
        cen = hg - mu
        var = jnp.mean(cen * cen, axis=-1, keepdims=True)
        hn = cen * lax.rsqrt(var + EPS) * mhw_ref[:, cols]
        zg = mz[:, cols]
        out = (hn + mskip_ref[:, cols] * xc_ref[:, cols]) * (zg * jax.nn.sigmoid(zg))
        mixed_ref[:, D_MODEL + h * GROUP_DIM:D_MODEL + (h + 1) * GROUP_DIM] = out.astype(bf16)

    y = x + _dot(mixed_ref[...], wout_ref[...])
    y = y * lax.rsqrt(jnp.mean(y * y, axis=-1, keepdims=True) + EPS)
    o_ref[...] = y * fnw_ref[...]


def _block_diag_groups(w, transpose=False):
    if transpose:
        w = jnp.swapaxes(w, 1, 2)
    w2 = w.reshape(D_MODEL, QKV_BLOCK)
    tiled = jnp.tile(w2, (1, GROUP_DIM // QKV_BLOCK))
    r = lax.broadcasted_iota(jnp.int32, (D_MODEL, GROUP_DIM), 0)
    s = lax.broadcasted_iota(jnp.int32, (D_MODEL, GROUP_DIM), 1)
    same_block = (r % GROUP_DIM) // QKV_BLOCK == s // QKV_BLOCK
    return jnp.where(same_block, tiled, 0.0).reshape(N_GROUPS, GROUP_DIM, GROUP_DIM)


def _const_spec(shape):
    zeros = (0,) * len(shape)
    return pl.BlockSpec(shape, lambda b, j: zeros, pipeline_mode=pl.Buffered(1))


def _pad_gate_cols(w):
    out = jnp.zeros((w.shape[0], GATE_LANES), jnp.float32)
    out = out.at[:, 0:N_GROUPS].set(w[:, :N_GROUPS])
    return out.at[:, F_GATE_ROW:F_GATE_ROW + N_GROUPS].set(w[:, N_GROUPS:])


@jax.jit
def kernel(x, norm_w, w_in, pool_w, pool_scale, conv_w, conv_b, w_q, w_k, w_v, w_if, b_if,
           mh_norm_w, m_skip, w_out, final_norm_w):
    batch, seq, d = x.shape
    assert d == D_MODEL and norm_w.shape[0] == 1 and seq % TILE_TOKENS == 0
    tm = TILE_TOKENS
    n_tiles = seq // tm
    bf16, f32 = jnp.bfloat16, jnp.float32

    k_scale = GROUP_DIM ** -0.5
    wq = _block_diag_groups(w_q[0]).astype(bf16)
    wkt = (_block_diag_groups(w_k[0], transpose=True) * k_scale).astype(bf16)
    wv = _block_diag_groups(w_v[0]).astype(bf16)
    wif = _pad_gate_cols(w_if[0])
    wifq = wif[0:D_MODEL].astype(bf16)
    wifkt = (wif[D_MODEL:2 * D_MODEL, 0:GATE_ROWS].T / k_scale).astype(bf16)
    wifv = wif[2 * D_MODEL:].astype(bf16)
    bif = _pad_gate_cols(b_if[0].reshape(1, -1))

    row = lambda a: a.reshape(1, -1).astype(f32)
    operands = (
        x.reshape(batch * seq, d), row(norm_w[0]), w_in[0].astype(bf16), pool_w[0].astype(bf16),
        row(pool_scale[0]), conv_w[0], row(conv_b[0]), wq, wkt, wv, wifq, wifkt, wifv, bif,
        row(mh_norm_w[0]), row(m_skip[0]), w_out[0].astype(bf16), row(final_norm_w))

    tile_spec = pl.BlockSpec((tm, d), lambda b, j: (b * n_tiles + j, 0))
    in_specs = [tile_spec] + [_const_spec(a.shape) for a in operands[1:]]

    out = pl.pallas_call(
        _fused_kernel,
        grid=(batch, n_tiles),
        in_specs=in_specs,
        out_specs=tile_spec,
        out_shape=jax.ShapeDtypeStruct((batch * seq, d), f32),
        scratch_shapes=[
            pltpu.VMEM((tm + POOL_HALO, d), f32),
            pltpu.VMEM((tm + CONV_HALO, d), f32),
            pltpu.VMEM((tm, d), f32),
            pltpu.VMEM((tm, d), bf16),
            pltpu.VMEM((d, tm), bf16),
            pltpu.VMEM((tm, d), bf16),
            pltpu.VMEM((tm, d), f32),
            pltpu.VMEM((tm, 2 * d), bf16),
            pltpu.VMEM((N_GROUPS, GROUP_DIM, GROUP_DIM), f32),
            pltpu.VMEM((N_GROUPS, GROUP_DIM, CHUNK), f32),
            pltpu.VMEM((8, CHUNK), f32),
        ],
        compiler_params=pltpu.CompilerParams(
            dimension_semantics=("arbitrary", "arbitrary"),
            vmem_limit_bytes=VMEM_LIMIT_BYTES),
        name="hybrid_pool_mlstm_block",
    )(*operands)
    return out.reshape(batch, seq, d)
```

```python
import jax
import jax.numpy as jnp
from jax import lax
from jax.experimental import pallas as pl
from jax.experimental.pallas import tpu as pltpu

D_MODEL = 1024
POOL_WINDOWS = (2, 4, 8, 16)
N_GROUPS = 4
GROUP_DIM = 256
QKV_BLOCK = 4
CONV_K = 4
CHUNK = 128
EPS = 1e-6

TILE_TOKENS = 256
POOL_HALO = 16
CONV_HALO = 8
SUBLANES = 8
LANES = 128
GATE_COLS = 2 * LANES
F_COPY_LANE = 8
VMEM_LIMIT_BYTES = 58 * 1024 * 1024

_NT = (((1,), (1,)), ((), ()))
_TN = (((0,), (0,)), ((), ()))


def _dot(a, b):
    return jnp.dot(a, b, preferred_element_type=jnp.float32)


def _sublane_scan(pieces, op, identity, sub):
    out, carry = [], None
    for v in pieces:
        for s in (1, 2, 4):
            v = op(v, jnp.where(sub >= s, pltpu.roll(v, s, axis=0), identity))
        if carry is not None:
            v = op(v, carry)
        carry = jnp.broadcast_to(v[SUBLANES - 1:SUBLANES, :], v.shape)
        out.append(v)
    return out


def _fused_kernel(x_ref, normw_ref, win_ref, poolw_ref, pscale_ref, convw_ref, convb_ref,
                  wq_ref, wkt_ref, wv_ref, wifq_ref, wifk_ref, wifv_ref, bif_ref,
                  mhw_ref, mskip_ref, wout_ref, fnw_ref,
                  o_ref,
                  px_ref, mxe_ref, xc_ref, q_ref, kt_ref, v_ref, hs_ref, mixed_ref,
                  ct_ref, n_ref, mcol_ref, mrow_ref, ax_ref, am_ref):
    tm = TILE_TOKENS
    f32, bf16 = jnp.float32, jnp.bfloat16
    j = pl.program_id(1)

    @pl.when((pl.program_id(0) == 0) & (j == 0))
    def _():
        for g in range(N_GROUPS):
            r = slice(g * GROUP_DIM, (g + 1) * GROUP_DIM)
            ax = _dot(wq_ref[g], wifq_ref[r, :]) + lax.dot_general(
                wkt_ref[g], wifk_ref[r, :], _TN, preferred_element_type=f32)
            ax_ref[r, :] = ax.astype(bf16)
            am_ref[r, :] = _dot(wv_ref[g], wifv_ref[r, :]).astype(bf16)

    @pl.when(j == 0)
    def _():
        px_ref[0:POOL_HALO, :] = jnp.zeros((POOL_HALO, D_MODEL), f32)
        mxe_ref[0:CONV_HALO, :] = jnp.zeros((CONV_HALO, D_MODEL), f32)
        ct_ref[...] = jnp.zeros(ct_ref.shape, f32)
        n_ref[...] = jnp.zeros(n_ref.shape, f32)
        mcol_ref[...] = jnp.zeros(mcol_ref.shape, f32)
        mrow_ref[...] = jnp.zeros(mrow_ref.shape, f32)

    x = x_ref[...]
    u = x * lax.rsqrt(jnp.mean(x * x, axis=-1, keepdims=True) + EPS)
    u = (u * normw_ref[...]).astype(bf16)

    def proj(k):
        return _dot(u, win_ref[:, k * D_MODEL:(k + 1) * D_MODEL])

    mxe_ref[CONV_HALO:CONV_HALO + tm, :] = proj(2)
    me = mxe_ref[...]
    conv = convb_ref[...] + convw_ref[CONV_K - 1:CONV_K, :] * me[CONV_HALO:, :]
    for back in range(1, CONV_K):
        tap = CONV_K - 1 - back
        conv = conv + convw_ref[tap:tap + 1, :] * pltpu.roll(me, back, axis=0)[CONV_HALO:, :]
    xc = conv * jax.nn.sigmoid(conv)
    xc_ref[...] = xc
    mxe_ref[0:CONV_HALO, :] = mxe_ref[tm:tm + CONV_HALO, :]
    xcb = xc.astype(bf16)
    mxb = me[CONV_HALO:, :].astype(bf16)

    gates = _dot(xcb, ax_ref[...]) + _dot(mxb, am_ref[...]) + bif_ref[...]
    ig_all = gates[:, :LANES]
    fp = gates[:, LANES:]
    logf_all = jnp.minimum(fp, 0.0) - jnp.log(1.0 + jnp.exp(-jnp.abs(fp)))

    sub = lax.broadcasted_iota(jnp.int32, (SUBLANES, LANES), 0)
    tri = (lax.broadcasted_iota(jnp.int32, (CHUNK, CHUNK), 0)
           >= lax.broadcasted_iota(jnp.int32, (CHUNK, CHUNK), 1))

    scans = []
    for c in range(tm // CHUNK):
        groups = [slice(c * CHUNK + SUBLANES * i, c * CHUNK + SUBLANES * (i + 1))
                  for i in range(CHUNK // SUBLANES)]
        b_p = _sublane_scan([logf_all[r, :] for r in groups], jnp.add, 0.0, sub)
        ub_p = [ig_all[r, :] - bp for r, bp in zip(groups, b_p)]
        cmax_p = _sublane_scan(ub_p, jnp.maximum, -jnp.inf, sub)
        b = jnp.concatenate(b_p, axis=0)
        ub = jnp.concatenate(ub_p, axis=0)
        cmax = jnp.concatenate(cmax_p, axis=0)
        ub_t = ub.T
        scans.append((b, cmax, ub_t[0:SUBLANES, :], ub_t[F_COPY_LANE:F_COPY_LANE + SUBLANES, :]))

    for g in range(N_GROUPS):
        cols = slice(g * GROUP_DIM, (g + 1) * GROUP_DIM)
        q_ref[:, cols] = _dot(xcb[:, cols], wq_ref[g]).astype(bf16)
        kt_ref[cols, :] = lax.dot_general(wkt_ref[g], xcb[:, cols], _NT,
                                          preferred_element_type=f32).astype(bf16)
        v_ref[:, cols] = _dot(mxb[:, cols], wv_ref[g]).astype(bf16)

    px_ref[POOL_HALO:POOL_HALO + tm, :] = proj(0)
    pos = j * tm + lax.broadcasted_iota(jnp.int32, (tm, LANES), 0)
    pz = proj(1)
    for g, win in enumerate(POOL_WINDOWS):
        cols = slice(g * GROUP_DIM, (g + 1) * GROUP_DIM)
        a = px_ref[:, cols]
        wsum, span = a, 1
        while span < win:
            wsum = wsum + pltpu.roll(wsum, span, axis=0)
            span *= 2
        inv = 1.0 / jnp.minimum(pos + 1, win).astype(f32)
        inv = jnp.concatenate([inv, inv], axis=1)
        d = wsum[POOL_HALO:, :] * inv - a[POOL_HALO:, :]
        y = _dot(d.astype(bf16), poolw_ref[g])
        zg = pz[:, cols]
        y = y * pscale_ref[:, cols] * (zg * jax.nn.sigmoid(zg))
        mixed_ref[:, cols] = y.astype(bf16)
    px_ref[0:POOL_HALO, :] = px_ref[tm:tm + POOL_HALO, :]

    def recurrence(c):
        rows = slice(c * CHUNK, (c + 1) * CHUNK)
        b, cmax, ub_rows, negb_rows = scans[c]
        m_row = mrow_ref[0:1, :]
        g_row = b[CHUNK - 1:CHUNK, :]
        stab = jnp.maximum(cmax, m_row)
        inter = jnp.exp(m_row - stab)
        exp_neg_mt = jnp.exp(-(b + stab))
        m_new_row = jnp.maximum(g_row + m_row, g_row + cmax[CHUNK - 1:CHUNK, :])
        mrow_ref[...] = jnp.broadcast_to(m_new_row, mrow_ref.shape)

        m_col = mcol_ref[...]
        g_col = -negb_rows[:, CHUNK - 1:CHUNK]
        m_new_col = jnp.maximum(g_col + m_col,
                                g_col + jnp.max(ub_rows, axis=1, keepdims=True))
        decay = jnp.exp(g_col + m_col - m_new_col)
        w_state = jnp.exp(g_col + ub_rows - m_new_col)
        mcol_ref[...] = m_new_col

        for h in range(N_GROUPS):
            cols = slice(h * GROUP_DIM, (h + 1) * GROUP_DIM)
            q = q_ref[rows, cols]
            kt = kt_ref[cols, rows]
            v = v_ref[rows, cols]
            stab_c = stab[:, h:h + 1]
            inter_c = inter[:, h:h + 1]
            emt_c = exp_neg_mt[:, h:h + 1]

            n_old = n_ref[h]
            s_ext = _dot(q, jnp.concatenate([kt, n_old.astype(bf16)], axis=1))
            e = jnp.where(tri, ub_rows[h:h + 1, :] - stab_c, -jnp.inf)
            wt = jnp.exp(e) * s_ext[:, :CHUNK]
            ct = ct_ref[h]
            num = _dot(wt.astype(bf16), v) + inter_c * _dot(q, ct.astype(bf16))
            den = jnp.sum(wt, axis=-1, keepdims=True) + inter_c * s_ext[:, CHUNK:CHUNK + 1]
            hs_ref[rows, cols] = num * (1.0 / jnp.maximum(jnp.abs(den), emt_c))

            ktw = kt.astype(f32) * w_state[h:h + 1, :]
            dec = decay[h:h + 1, 0:1]
            ct_ref[h] = dec * ct + _dot(ktw.astype(bf16), v)
            n_ref[h] = dec * n_old + jnp.sum(ktw, axis=1, keepdims=True)

    recurrence(0)
    mz = proj(3)
    mo = proj(4)
    for c in range(1, tm // CHUNK):
        recurrence(c)

    for h in range(N_GROUPS):
        cols = slice(h * GROUP_DIM, (h + 1) * GROUP_DIM)
        hg = jax.nn.sigmoid(mo[:, cols]) * hs_ref[:, cols]
        mu = jnp.mean(hg, axis=-1, keepdims=True)
        cen = hg - mu
        var = jnp.mean(cen * cen, axis=-1, keepdims=True)
        hn = cen * lax.rsqrt(var + EPS) * mhw_ref[:, cols]
        zg = mz[:, cols]
        out = (hn + mskip_ref[:, cols] * xc_ref[:, cols]) * (zg * jax.nn.sigmoid(zg))
        mixed_ref[:, D_MODEL + h * GROUP_DIM:D_MODEL + (h + 1) * GROUP_DIM] = out.astype(bf16)

    y = x + _dot(mixed_ref[...], wout_ref[...])
    y = y * lax.rsqrt(jnp.mean(y * y, axis=-1, keepdims=True) + EPS)
    o_ref[...] = y * fnw_ref[...]


def _block_diag_groups(w, transpose=False):
    if transpose:
        w = jnp.swapaxes(w, 1, 2)
    w2 = w.reshape(D_MODEL, QKV_BLOCK)
    tiled = jnp.tile(w2, (1, GROUP_DIM // QKV_BLOCK))
    r = lax.broadcasted_iota(jnp.int32, (D_MODEL, GROUP_DIM), 0)
    s = lax.broadcasted_iota(jnp.int32, (D_MODEL, GROUP_DIM), 1)
    same_block = (r % GROUP_DIM) // QKV_BLOCK == s // QKV_BLOCK
    return jnp.where(same_block, tiled, 0.0).reshape(N_GROUPS, GROUP_DIM, GROUP_DIM)


def _const_spec(shape):
    zeros = (0,) * len(shape)
    return pl.BlockSpec(shape, lambda b, j: zeros, pipeline_mode=pl.Buffered(1))


def _gate_layout(w):
    rows = w.shape[0]
    wi, wf = w[:, :N_GROUPS], w[:, N_GROUPS:]
    z = lambda n: jnp.zeros((rows, n), w.dtype)
    return jnp.concatenate(
        [wi, z(LANES - N_GROUPS),
         wf, z(F_COPY_LANE - N_GROUPS), wf, z(LANES - F_COPY_LANE - N_GROUPS)], axis=1)


@jax.jit
def kernel(x, norm_w, w_in, pool_w, pool_scale, conv_w, conv_b, w_q, w_k, w_v, w_if, b_if,
           mh_norm_w, m_skip, w_out, final_norm_w):
    batch, seq, d = x.shape
    assert d == D_MODEL and norm_w.shape[0] == 1 and seq % TILE_TOKENS == 0
    tm = TILE_TOKENS
    n_tiles = seq // tm
    bf16, f32 = jnp.bfloat16, jnp.float32

    k_scale = GROUP_DIM ** -0.5
    wq = _block_diag_groups(w_q[0]).astype(bf16)
    wkt = (_block_diag_groups(w_k[0], transpose=True) * k_scale).astype(bf16)
    wv = _block_diag_groups(w_v[0]).astype(bf16)
    wif = _gate_layout(w_if[0])
    wifq = wif[0:D_MODEL].astype(bf16)
    wifk = (wif[D_MODEL:2 * D_MODEL] / k_scale).astype(bf16)
    wifv = wif[2 * D_MODEL:].astype(bf16)
    bif = _gate_layout(b_if[0].reshape(1, -1))

    row = lambda a: a.reshape(1, -1).astype(f32)
    operands = (
        x.reshape(batch * seq, d), row(norm_w[0]), w_in[0].astype(bf16), pool_w[0].astype(bf16),
        row(pool_scale[0]), conv_w[0], row(conv_b[0]), wq, wkt, wv, wifq, wifk, wifv, bif,
        row(mh_norm_w[0]), row(m_skip[0]), w_out[0].astype(bf16), row(final_norm_w))

    tile_spec = pl.BlockSpec((tm, d), lambda b, j: (b * n_tiles + j, 0))
    in_specs = [tile_spec] + [_const_spec(a.shape) for a in operands[1:]]

    out = pl.pallas_call(
        _fused_kernel,
        grid=(batch, n_tiles),
        in_specs=in_specs,
        out_specs=tile_spec,
        out_shape=jax.ShapeDtypeStruct((batch * seq, d), f32),
        scratch_shapes=[
            pltpu.VMEM((tm + POOL_HALO, d), f32),
            pltpu.VMEM((tm + CONV_HALO, d), f32),
            pltpu.VMEM((tm, d), f32),
            pltpu.VMEM((tm, d), bf16),
            pltpu.VMEM((d, tm), bf16),
            pltpu.VMEM((tm, d), bf16),
            pltpu.VMEM((tm, d), f32),
            pltpu.VMEM((tm, 2 * d), bf16),
            pltpu.VMEM((N_GROUPS, GROUP_DIM, GROUP_DIM), f32),
            pltpu.VMEM((N_GROUPS, GROUP_DIM, CHUNK), f32),
            pltpu.VMEM((SUBLANES, LANES), f32),
            pltpu.VMEM((SUBLANES, LANES), f32),
            pltpu.VMEM((d, GATE_COLS), bf16),
            pltpu.VMEM((d, GATE_COLS), bf16),
        ],
        compiler_params=pltpu.CompilerParams(
            dimension_semantics=("arbitrary", "arbitrary"),
            vmem_limit_bytes=VMEM_LIMIT_BYTES),
        name="hybrid_pool_mlstm_block",
    )(*operands)
    return out.reshape(batch, seq, d)
```

```python
import functools

import jax
import jax.numpy as jnp
from jax import lax
from jax.experimental import pallas as pl
from jax.experimental.pallas import tpu as pltpu

D_MODEL = 1024
POOL_WINDOWS = (2, 4, 8, 16)
N_GROUPS = 4
GROUP_DIM = 256
QKV_BLOCK = 4
CONV_K = 4
CHUNK = 128
EPS = 1e-6

TILE_TOKENS = 256
POOL_HALO = 16
CONV_HALO = 8
SUBLANES = 8
LANES = 128
GATE_COLS = 2 * LANES
F_COPY_LANE = 8
VMEM_LIMIT_BYTES = 58 * 1024 * 1024

_NT = (((1,), (1,)), ((), ()))
_TN = (((0,), (0,)), ((), ()))


def _dot(a, b):
    return jnp.dot(a, b, preferred_element_type=jnp.float32)


def _sublane_scan(pieces, op, identity, sub):
    out, carry = [], None
    for v in pieces:
        for s in (1, 2, 4):
            v = op(v, jnp.where(sub >= s, pltpu.roll(v, s, axis=0), identity))
        if carry is not None:
            v = op(v, carry)
        carry = jnp.broadcast_to(v[SUBLANES - 1:SUBLANES, :], v.shape)
        out.append(v)
    return out


def _fused_kernel(tiles_per_seq,
                  x_ref, normw_ref, win_ref, poolw_ref, pscale_ref, convw_ref,
                  convb_ref, wq_ref, wkt_ref, wv_ref, wifq_ref, wifk_ref, wifv_ref, bif_ref,
                  mhw_ref, mskip_ref, wout_ref, fnw_ref,
                  o_ref,
                  px_ref, mxe_ref, q_ref, kt_ref, v_ref,
                  xc_ref, mz_ref, mo_ref, hs_ref, resid_ref,
                  ct_ref, n_ref, mcol_ref, mrow_ref, ax_ref, am_ref):
    tm = TILE_TOKENS
    f32, bf16 = jnp.float32, jnp.bfloat16
    i = pl.program_id(0)
    j = lax.rem(i, tiles_per_seq)

    @pl.when(i == 0)
    def _():
        for g in range(N_GROUPS):
            r = slice(g * GROUP_DIM, (g + 1) * GROUP_DIM)
            ax = _dot(wq_ref[g], wifq_ref[r, :]) + lax.dot_general(
                wkt_ref[g], wifk_ref[r, :], _TN, preferred_element_type=f32)
            ax_ref[r, :] = ax.astype(bf16)
            am_ref[r, :] = _dot(wv_ref[g], wifv_ref[r, :]).astype(bf16)
        xc_ref[...] = jnp.zeros(xc_ref.shape, f32)
        mz_ref[...] = jnp.zeros(mz_ref.shape, f32)
        mo_ref[...] = jnp.zeros(mo_ref.shape, f32)
        hs_ref[...] = jnp.zeros(hs_ref.shape, f32)
        resid_ref[...] = jnp.zeros(resid_ref.shape, f32)

    @pl.when(j == 0)
    def _():
        px_ref[0:POOL_HALO, :] = jnp.zeros((POOL_HALO, D_MODEL), f32)
        mxe_ref[0:CONV_HALO, :] = jnp.zeros((CONV_HALO, D_MODEL), f32)
        ct_ref[...] = jnp.zeros(ct_ref.shape, f32)
        n_ref[...] = jnp.zeros(n_ref.shape, f32)
        mcol_ref[...] = jnp.zeros(mcol_ref.shape, f32)
        mrow_ref[...] = jnp.zeros(mrow_ref.shape, f32)

    resid_prev = resid_ref[...]
    y_mlstm = []
    for h in range(N_GROUPS):
        cols = slice(h * GROUP_DIM, (h + 1) * GROUP_DIM)
        hg = jax.nn.sigmoid(mo_ref[:, cols]) * hs_ref[:, cols]
        mu = jnp.mean(hg, axis=-1, keepdims=True)
        cen = hg - mu
        var = jnp.mean(cen * cen, axis=-1, keepdims=True)
        hn = cen * lax.rsqrt(var + EPS) * mhw_ref[:, cols]
        zg = mz_ref[:, cols]
        out = (hn + mskip_ref[:, cols] * xc_ref[:, cols]) * (zg * jax.nn.sigmoid(zg))
        y_mlstm.append(out.astype(bf16))
    y_mlstm = jnp.concatenate(y_mlstm, axis=1)

    x = x_ref[...]
    u = x * lax.rsqrt(jnp.mean(x * x, axis=-1, keepdims=True) + EPS)
    u = (u * normw_ref[...]).astype(bf16)

    def proj(k):
        return _dot(u, win_ref[:, k * D_MODEL:(k + 1) * D_MODEL])

    mxe_ref[CONV_HALO:CONV_HALO + tm, :] = proj(2)
    me = mxe_ref[...]
    conv = convb_ref[...] + convw_ref[CONV_K - 1:CONV_K, :] * me[CONV_HALO:, :]
    for back in range(1, CONV_K):
        tap = CONV_K - 1 - back
        conv = conv + convw_ref[tap:tap + 1, :] * pltpu.roll(me, back, axis=0)[CONV_HALO:, :]
    xc = conv * jax.nn.sigmoid(conv)
    xc_ref[...] = xc
    mxe_ref[0:CONV_HALO, :] = mxe_ref[tm:tm + CONV_HALO, :]
    xcb = xc.astype(bf16)
    mxb = me[CONV_HALO:, :].astype(bf16)

    gates = _dot(xcb, ax_ref[...]) + _dot(mxb, am_ref[...]) + bif_ref[...]
    ig_all = gates[:, :LANES]
    fp = gates[:, LANES:]
    logf_all = jnp.minimum(fp, 0.0) - jnp.log(1.0 + jnp.exp(-jnp.abs(fp)))

    sub = lax.broadcasted_iota(jnp.int32, (SUBLANES, LANES), 0)
    tri = (lax.broadcasted_iota(jnp.int32, (CHUNK, CHUNK), 0)
           >= lax.broadcasted_iota(jnp.int32, (CHUNK, CHUNK), 1))

    scans = []
    for c in range(tm // CHUNK):
        groups = [slice(c * CHUNK + SUBLANES * r, c * CHUNK + SUBLANES * (r + 1))
                  for r in range(CHUNK // SUBLANES)]
        b_p = _sublane_scan([logf_all[r, :] for r in groups], jnp.add, 0.0, sub)
        ub_p = [ig_all[r, :] - bp for r, bp in zip(groups, b_p)]
        cmax_p = _sublane_scan(ub_p, jnp.maximum, -jnp.inf, sub)
        b = jnp.concatenate(b_p, axis=0)
        ub = jnp.concatenate(ub_p, axis=0)
        cmax = jnp.concatenate(cmax_p, axis=0)
        ub_t = ub.T
        scans.append((b, cmax, ub_t[0:SUBLANES, :], ub_t[F_COPY_LANE:F_COPY_LANE + SUBLANES, :]))

    for g in range(N_GROUPS):
        cols = slice(g * GROUP_DIM, (g + 1) * GROUP_DIM)
        q_ref[:, cols] = _dot(xcb[:, cols], wq_ref[g]).astype(bf16)
        kt_ref[cols, :] = lax.dot_general(wkt_ref[g], xcb[:, cols], _NT,
                                          preferred_element_type=f32).astype(bf16)
        v_ref[:, cols] = _dot(mxb[:, cols], wv_ref[g]).astype(bf16)

    px_ref[POOL_HALO:POOL_HALO + tm, :] = proj(0)
    pos = j * tm + lax.broadcasted_iota(jnp.int32, (tm, LANES), 0)
    pz = proj(1)
    resid = x
    for g, win in enumerate(POOL_WINDOWS):
        cols = slice(g * GROUP_DIM, (g + 1) * GROUP_DIM)
        a = px_ref[:, cols]
        wsum, span = a, 1
        while span < win:
            wsum = wsum + pltpu.roll(wsum, span, axis=0)
            span *= 2
        inv = 1.0 / jnp.minimum(pos + 1, win).astype(f32)
        inv = jnp.concatenate([inv, inv], axis=1)
        d = wsum[POOL_HALO:, :] * inv - a[POOL_HALO:, :]
        y = _dot(d.astype(bf16), poolw_ref[g])
        zg = pz[:, cols]
        y = y * pscale_ref[:, cols] * (zg * jax.nn.sigmoid(zg))
        resid = resid + _dot(y.astype(bf16), wout_ref[cols, :])
    resid_ref[...] = resid
    px_ref[0:POOL_HALO, :] = px_ref[tm:tm + POOL_HALO, :]

    def recurrence(c):
        rows = slice(c * CHUNK, (c + 1) * CHUNK)
        b, cmax, ub_rows, negb_rows = scans[c]
        m_row = mrow_ref[0:1, :]
        g_row = b[CHUNK - 1:CHUNK, :]
        stab = jnp.maximum(cmax, m_row)
        inter = jnp.exp(m_row - stab)
        exp_neg_mt = jnp.exp(-(b + stab))
        m_new_row = jnp.maximum(g_row + m_row, g_row + cmax[CHUNK - 1:CHUNK, :])
        mrow_ref[...] = jnp.broadcast_to(m_new_row, mrow_ref.shape)

        m_col = mcol_ref[...]
        g_col = -negb_rows[:, CHUNK - 1:CHUNK]
        m_new_col = jnp.maximum(g_col + m_col,
                                g_col + jnp.max(ub_rows, axis=1, keepdims=True))
        decay = jnp.exp(g_col + m_col - m_new_col)
        w_state = jnp.exp(g_col + ub_rows - m_new_col)
        mcol_ref[...] = m_new_col

        for h in range(N_GROUPS):
            cols = slice(h * GROUP_DIM, (h + 1) * GROUP_DIM)
            q = q_ref[rows, cols]
            kt = kt_ref[cols, rows]
            v = v_ref[rows, cols]
            stab_c = stab[:, h:h + 1]
            inter_c = inter[:, h:h + 1]
            emt_c = exp_neg_mt[:, h:h + 1]

            n_old = n_ref[h]
            s_ext = _dot(q, jnp.concatenate([kt, n_old.astype(bf16)], axis=1))
            e = jnp.where(tri, ub_rows[h:h + 1, :] - stab_c, -jnp.inf)
            wt = jnp.exp(e) * s_ext[:, :CHUNK]
            ct = ct_ref[h]
            num = _dot(wt.astype(bf16), v) + inter_c * _dot(q, ct.astype(bf16))
            den = jnp.sum(wt, axis=-1, keepdims=True) + inter_c * s_ext[:, CHUNK:CHUNK + 1]
            hs_ref[rows, cols] = num * (1.0 / jnp.maximum(jnp.abs(den), emt_c))

            ktw = kt.astype(f32) * w_state[h:h + 1, :]
            dec = decay[h:h + 1, 0:1]
            ct_ref[h] = dec * ct + _dot(ktw.astype(bf16), v)
            n_ref[h] = dec * n_old + jnp.sum(ktw, axis=1, keepdims=True)

    recurrence(0)
    mz_ref[...] = proj(3)
    mo_ref[...] = proj(4)
    for c in range(1, tm // CHUNK):
        recurrence(c)

    y = resid_prev + _dot(y_mlstm, wout_ref[D_MODEL:2 * D_MODEL, :])
    y = y * lax.rsqrt(jnp.mean(y * y, axis=-1, keepdims=True) + EPS)
    o_ref[...] = y * fnw_ref[...]


def _block_diag_groups(w, transpose=False):
    if transpose:
        w = jnp.swapaxes(w, 1, 2)
    w2 = w.reshape(D_MODEL, QKV_BLOCK)
    tiled = jnp.tile(w2, (1, GROUP_DIM // QKV_BLOCK))
    r = lax.broadcasted_iota(jnp.int32, (D_MODEL, GROUP_DIM), 0)
    s = lax.broadcasted_iota(jnp.int32, (D_MODEL, GROUP_DIM), 1)
    same_block = (r % GROUP_DIM) // QKV_BLOCK == s // QKV_BLOCK
    return jnp.where(same_block, tiled, 0.0).reshape(N_GROUPS, GROUP_DIM, GROUP_DIM)


def _const_spec(shape):
    zeros = (0,) * len(shape)
    return pl.BlockSpec(shape, lambda i: zeros, pipeline_mode=pl.Buffered(1))


def _gate_layout(w):
    rows = w.shape[0]
    wi, wf = w[:, :N_GROUPS], w[:, N_GROUPS:]
    z = lambda n: jnp.zeros((rows, n), w.dtype)
    return jnp.concatenate(
        [wi, z(LANES - N_GROUPS),
         wf, z(F_COPY_LANE - N_GROUPS), wf, z(LANES - F_COPY_LANE - N_GROUPS)], axis=1)


@jax.jit
def kernel(x, norm_w, w_in, pool_w, pool_scale, conv_w, conv_b, w_q, w_k, w_v, w_if, b_if,
           mh_norm_w, m_skip, w_out, final_norm_w):
    batch, seq, d = x.shape
    assert d == D_MODEL and norm_w.shape[0] == 1 and seq % TILE_TOKENS == 0
    tm = TILE_TOKENS
    tiles_per_seq = seq // tm
    n_tiles = batch * tiles_per_seq
    bf16, f32 = jnp.bfloat16, jnp.float32

    k_scale = GROUP_DIM ** -0.5
    wq = _block_diag_groups(w_q[0]).astype(bf16)
    wkt = (_block_diag_groups(w_k[0], transpose=True) * k_scale).astype(bf16)
    wv = _block_diag_groups(w_v[0]).astype(bf16)
    wif = _gate_layout(w_if[0])
    wifq = wif[0:D_MODEL].astype(bf16)
    wifk = (wif[D_MODEL:2 * D_MODEL] / k_scale).astype(bf16)
    wifv = wif[2 * D_MODEL:].astype(bf16)
    bif = _gate_layout(b_if[0].reshape(1, -1))

    row = lambda a: a.reshape(1, -1).astype(f32)
    x2 = x.reshape(batch * seq, d)
    operands = (
        x2, row(norm_w[0]), w_in[0].astype(bf16), pool_w[0].astype(bf16),
        row(pool_scale[0]), conv_w[0], row(conv_b[0]), wq, wkt, wv, wifq, wifk, wifv, bif,
        row(mh_norm_w[0]), row(m_skip[0]), w_out[0].astype(bf16), row(final_norm_w))

    front_spec = pl.BlockSpec((tm, d), lambda i: (jnp.minimum(i, n_tiles - 1), 0))
    back_spec = pl.BlockSpec((tm, d), lambda i: (jnp.maximum(i - 1, 0), 0))
    in_specs = [front_spec] + [_const_spec(a.shape) for a in operands[1:]]

    out = pl.pallas_call(
        functools.partial(_fused_kernel, tiles_per_seq),
        grid=(n_tiles + 1,),
        in_specs=in_specs,
        out_specs=back_spec,
        out_shape=jax.ShapeDtypeStruct((batch * seq, d), f32),
        scratch_shapes=[
            pltpu.VMEM((tm + POOL_HALO, d), f32),
            pltpu.VMEM((tm + CONV_HALO, d), f32),
            pltpu.VMEM((tm, d), bf16),
            pltpu.VMEM((d, tm), bf16),
            pltpu.VMEM((tm, d), bf16),
            pltpu.VMEM((tm, d), f32),
            pltpu.VMEM((tm, d), f32),
            pltpu.VMEM((tm, d), f32),
            pltpu.VMEM((tm, d), f32),
            pltpu.VMEM((tm, d), f32),
            pltpu.VMEM((N_GROUPS, GROUP_DIM, GROUP_DIM), f32),
            pltpu.VMEM((N_GROUPS, GROUP_DIM, CHUNK), f32),
            pltpu.VMEM((SUBLANES, LANES), f32),
            pltpu.VMEM((SUBLANES, LANES), f32),
            pltpu.VMEM((d, GATE_COLS), bf16),
            pltpu.VMEM((d, GATE_COLS), bf16),
        ],
        compiler_params=pltpu.CompilerParams(
            dimension_semantics=("arbitrary",),
            vmem_limit_bytes=VMEM_LIMIT_BYTES),
        name="hybrid_pool_mlstm_block",
    )(*operands)
    return out.reshape(batch, seq, d)
```

```python
import functools

import jax
import jax.numpy as jnp
from jax import lax
from jax.experimental import pallas as pl
from jax.experimental.pallas import tpu as pltpu

D_MODEL = 1024
POOL_WINDOWS = (2, 4, 8, 16)
N_GROUPS = 4
GROUP_DIM = 256
QKV_BLOCK = 4
CONV_K = 4
CHUNK = 128
EPS = 1e-6

TILE_TOKENS = 256
POOL_HALO = 16
CONV_HALO = 8
SUBLANES = 8
LANES = 128
GATE_COLS = 2 * LANES
F_COPY_LANE = 8
SLAB_COLS = 256
SLABS_PER_D = D_MODEL // SLAB_COLS
VMEM_LIMIT_BYTES = 58 * 1024 * 1024

_NT = (((1,), (1,)), ((), ()))
_TN = (((0,), (0,)), ((), ()))


def _dot(a, b):
    return jnp.dot(a, b, preferred_element_type=jnp.float32)


def _dot_slabs(a, w_ref, first_slab, k_lo, k_hi):
    return jnp.concatenate(
        [_dot(a, w_ref[first_slab + s, k_lo:k_hi, :]) for s in range(SLABS_PER_D)], axis=1)


def _sublane_scan(pieces, op, identity, sub):
    out, carry = [], None
    for v in pieces:
        for s in (1, 2, 4):
            v = op(v, jnp.where(sub >= s, pltpu.roll(v, s, axis=0), identity))
        if carry is not None:
            v = op(v, carry)
        carry = jnp.broadcast_to(v[SUBLANES - 1:SUBLANES, :], v.shape)
        out.append(v)
    return out


def _fused_kernel(tiles_per_seq,
                  x_ref, normw_ref, win_ref, poolw_ref, pscale_ref, convw_ref,
                  convb_ref, wq_ref, wkt_ref, wv_ref, wifq_ref, wifk_ref, wifv_ref, bif_ref,
                  mhw_ref, mskip_ref, wout_ref, fnw_ref,
                  o_ref,
                  px_ref, mxe_ref, q_ref, kt_ref, v_ref,
                  xc_ref, mz_ref, mo_ref, hs_ref, resid_ref,
                  ct_ref, n_ref, mcol_ref, mrow_ref, ax_ref, am_ref):
    tm = TILE_TOKENS
    f32, bf16 = jnp.float32, jnp.bfloat16
    i = pl.program_id(0)
    j = lax.rem(i, tiles_per_seq)

    @pl.when(i == 0)
    def _():
        for g in range(N_GROUPS):
            r = slice(g * GROUP_DIM, (g + 1) * GROUP_DIM)
            ax = _dot(wq_ref[g], wifq_ref[r, :]) + lax.dot_general(
                wkt_ref[g], wifk_ref[r, :], _TN, preferred_element_type=f32)
            ax_ref[r, :] = ax.astype(bf16)
            am_ref[r, :] = _dot(wv_ref[g], wifv_ref[r, :]).astype(bf16)
        xc_ref[...] = jnp.zeros(xc_ref.shape, f32)
        mz_ref[...] = jnp.zeros(mz_ref.shape, f32)
        mo_ref[...] = jnp.zeros(mo_ref.shape, f32)
        hs_ref[...] = jnp.zeros(hs_ref.shape, f32)
        resid_ref[...] = jnp.zeros(resid_ref.shape, f32)

    @pl.when(j == 0)
    def _():
        px_ref[0:POOL_HALO, :] = jnp.zeros((POOL_HALO, D_MODEL), f32)
        mxe_ref[0:CONV_HALO, :] = jnp.zeros((CONV_HALO, D_MODEL), f32)
        ct_ref[...] = jnp.zeros(ct_ref.shape, f32)
        n_ref[...] = jnp.zeros(n_ref.shape, f32)
        mcol_ref[...] = jnp.zeros(mcol_ref.shape, f32)
        mrow_ref[...] = jnp.zeros(mrow_ref.shape, f32)

    resid_prev = resid_ref[...]
    y_mlstm = []
    for h in range(N_GROUPS):
        cols = slice(h * GROUP_DIM, (h + 1) * GROUP_DIM)
        hg = jax.nn.sigmoid(mo_ref[:, cols]) * hs_ref[:, cols]
        mu = jnp.mean(hg, axis=-1, keepdims=True)
        cen = hg - mu
        var = jnp.mean(cen * cen, axis=-1, keepdims=True)
        hn = cen * lax.rsqrt(var + EPS) * mhw_ref[:, cols]
        zg = mz_ref[:, cols]
        out = (hn + mskip_ref[:, cols] * xc_ref[:, cols]) * (zg * jax.nn.sigmoid(zg))
        y_mlstm.append(out.astype(bf16))
    y_mlstm = jnp.concatenate(y_mlstm, axis=1)

    x = x_ref[...]
    u = x * lax.rsqrt(jnp.mean(x * x, axis=-1, keepdims=True) + EPS)
    u = (u * normw_ref[...]).astype(bf16)

    def proj(k):
        return _dot_slabs(u, win_ref, k * SLABS_PER_D, 0, D_MODEL)

    mxe_ref[CONV_HALO:CONV_HALO + tm, :] = proj(2)
    me = mxe_ref[...]
    conv = convb_ref[...] + convw_ref[CONV_K - 1:CONV_K, :] * me[CONV_HALO:, :]
    for back in range(1, CONV_K):
        tap = CONV_K - 1 - back
        conv = conv + convw_ref[tap:tap + 1, :] * pltpu.roll(me, back, axis=0)[CONV_HALO:, :]
    xc = conv * jax.nn.sigmoid(conv)
    xc_ref[...] = xc
    mxe_ref[0:CONV_HALO, :] = mxe_ref[tm:tm + CONV_HALO, :]
    xcb = xc.astype(bf16)
    mxb = me[CONV_HALO:, :].astype(bf16)

    gates = _dot(xcb, ax_ref[...]) + _dot(mxb, am_ref[...]) + bif_ref[...]
    ig_all = gates[:, :LANES]
    fp = gates[:, LANES:]
    logf_all = jnp.minimum(fp, 0.0) - jnp.log(1.0 + jnp.exp(-jnp.abs(fp)))

    sub = lax.broadcasted_iota(jnp.int32, (SUBLANES, LANES), 0)
    tri = (lax.broadcasted_iota(jnp.int32, (CHUNK, CHUNK), 0)
           >= lax.broadcasted_iota(jnp.int32, (CHUNK, CHUNK), 1))

    scans = []
    for c in range(tm // CHUNK):
        groups = [slice(c * CHUNK + SUBLANES * r, c * CHUNK + SUBLANES * (r + 1))
                  for r in range(CHUNK // SUBLANES)]
        b_p = _sublane_scan([logf_all[r, :] for r in groups], jnp.add, 0.0, sub)
        ub_p = [ig_all[r, :] - bp for r, bp in zip(groups, b_p)]
        cmax_p = _sublane_scan(ub_p, jnp.maximum, -jnp.inf, sub)
        b = jnp.concatenate(b_p, axis=0)
        ub = jnp.concatenate(ub_p, axis=0)
        cmax = jnp.concatenate(cmax_p, axis=0)
        ub_t = ub.T
        scans.append((b, cmax, ub_t[0:SUBLANES, :], ub_t[F_COPY_LANE:F_COPY_LANE + SUBLANES, :]))

    for g in range(N_GROUPS):
        cols = slice(g * GROUP_DIM, (g + 1) * GROUP_DIM)
        q_ref[:, cols] = _dot(xcb[:, cols], wq_ref[g]).astype(bf16)
        kt_ref[cols, :] = lax.dot_general(wkt_ref[g], xcb[:, cols], _NT,
                                          preferred_element_type=f32).astype(bf16)
        v_ref[:, cols] = _dot(mxb[:, cols], wv_ref[g]).astype(bf16)

    px_ref[POOL_HALO:POOL_HALO + tm, :] = proj(0)
    pos = j * tm + lax.broadcasted_iota(jnp.int32, (tm, LANES), 0)
    pz = proj(1)
    y_pool = []
    for g, win in enumerate(POOL_WINDOWS):
        cols = slice(g * GROUP_DIM, (g + 1) * GROUP_DIM)
        a = px_ref[:, cols]
        wsum, span = a, 1
        while span < win:
            wsum = wsum + pltpu.roll(wsum, span, axis=0)
            span *= 2
        inv = 1.0 / jnp.minimum(pos + 1, win).astype(f32)
        inv = jnp.concatenate([inv, inv], axis=1)
        d = wsum[POOL_HALO:, :] * inv - a[POOL_HALO:, :]
        y = _dot(d.astype(bf16), poolw_ref[g])
        zg = pz[:, cols]
        y = y * pscale_ref[:, cols] * (zg * jax.nn.sigmoid(zg))
        y_pool.append(y.astype(bf16))
    y_pool = jnp.concatenate(y_pool, axis=1)
    resid_ref[...] = x + _dot_slabs(y_pool, wout_ref, 0, 0, D_MODEL)
    px_ref[0:POOL_HALO, :] = px_ref[tm:tm + POOL_HALO, :]

    def recurrence(c):
        rows = slice(c * CHUNK, (c + 1) * CHUNK)
        b, cmax, ub_rows, negb_rows = scans[c]
        m_row = mrow_ref[0:1, :]
        g_row = b[CHUNK - 1:CHUNK, :]
        stab = jnp.maximum(cmax, m_row)
        inter = jnp.exp(m_row - stab)
        exp_neg_mt = jnp.exp(-(b + stab))
        m_new_row = jnp.maximum(g_row + m_row, g_row + cmax[CHUNK - 1:CHUNK, :])
        mrow_ref[...] = jnp.broadcast_to(m_new_row, mrow_ref.shape)

        m_col = mcol_ref[...]
        g_col = -negb_rows[:, CHUNK - 1:CHUNK]
        m_new_col = jnp.maximum(g_col + m_col,
                                g_col + jnp.max(ub_rows, axis=1, keepdims=True))
        decay = jnp.exp(g_col + m_col - m_new_col)
        w_state = jnp.exp(g_col + ub_rows - m_new_col)
        mcol_ref[...] = m_new_col

        for h in range(N_GROUPS):
            cols = slice(h * GROUP_DIM, (h + 1) * GROUP_DIM)
            q = q_ref[rows, cols]
            kt = kt_ref[cols, rows]
            v = v_ref[rows, cols]
            stab_c = stab[:, h:h + 1]
            inter_c = inter[:, h:h + 1]
            emt_c = exp_neg_mt[:, h:h + 1]

            n_old = n_ref[h]
            s_ext = _dot(q, jnp.concatenate([kt, n_old.astype(bf16)], axis=1))
            e = jnp.where(tri, ub_rows[h:h + 1, :] - stab_c, -jnp.inf)
            wt = jnp.exp(e) * s_ext[:, :CHUNK]
            ct = ct_ref[h]
            num = _dot(wt.astype(bf16), v) + inter_c * _dot(q, ct.astype(bf16))
            den = jnp.sum(wt, axis=-1, keepdims=True) + inter_c * s_ext[:, CHUNK:CHUNK + 1]
            hs_ref[rows, cols] = num * (1.0 / jnp.maximum(jnp.abs(den), emt_c))

            ktw = kt.astype(f32) * w_state[h:h + 1, :]
            dec = decay[h:h + 1, 0:1]
            ct_ref[h] = dec * ct + _dot(ktw.astype(bf16), v)
            n_ref[h] = dec * n_old + jnp.sum(ktw, axis=1, keepdims=True)

    recurrence(0)
    mz_ref[...] = proj(3)
    mo_ref[...] = proj(4)
    for c in range(1, tm // CHUNK):
        recurrence(c)

    y = resid_prev + _dot_slabs(y_mlstm, wout_ref, 0, D_MODEL, 2 * D_MODEL)
    y = y * lax.rsqrt(jnp.mean(y * y, axis=-1, keepdims=True) + EPS)
    o_ref[...] = y * fnw_ref[...]


def _block_diag_groups(w, transpose=False):
    if transpose:
        w = jnp.swapaxes(w, 1, 2)
    w2 = w.reshape(D_MODEL, QKV_BLOCK)
    tiled = jnp.tile(w2, (1, GROUP_DIM // QKV_BLOCK))
    r = lax.broadcasted_iota(jnp.int32, (D_MODEL, GROUP_DIM), 0)
    s = lax.broadcasted_iota(jnp.int32, (D_MODEL, GROUP_DIM), 1)
    same_block = (r % GROUP_DIM) // QKV_BLOCK == s // QKV_BLOCK
    return jnp.where(same_block, tiled, 0.0).reshape(N_GROUPS, GROUP_DIM, GROUP_DIM)


def _col_slabs(w):
    k, n = w.shape
    return w.astype(jnp.bfloat16).reshape(k, n // SLAB_COLS, SLAB_COLS).transpose(1, 0, 2)


def _const_spec(shape):
    zeros = (0,) * len(shape)
    return pl.BlockSpec(shape, lambda i: zeros, pipeline_mode=pl.Buffered(1))


def _gate_layout(w):
    rows = w.shape[0]
    wi, wf = w[:, :N_GROUPS], w[:, N_GROUPS:]
    z = lambda n: jnp.zeros((rows, n), w.dtype)
    return jnp.concatenate(
        [wi, z(LANES - N_GROUPS),
         wf, z(F_COPY_LANE - N_GROUPS), wf, z(LANES - F_COPY_LANE - N_GROUPS)], axis=1)


@jax.jit
def kernel(x, norm_w, w_in, pool_w, pool_scale, conv_w, conv_b, w_q, w_k, w_v, w_if, b_if,
           mh_norm_w, m_skip, w_out, final_norm_w):
    batch, seq, d = x.shape
    assert d == D_MODEL and norm_w.shape[0] == 1 and seq % TILE_TOKENS == 0
    tm = TILE_TOKENS
    tiles_per_seq = seq // tm
    n_tiles = batch * tiles_per_seq
    bf16, f32 = jnp.bfloat16, jnp.float32

    k_scale = GROUP_DIM ** -0.5
    wq = _block_diag_groups(w_q[0]).astype(bf16)
    wkt = (_block_diag_groups(w_k[0], transpose=True) * k_scale).astype(bf16)
    wv = _block_diag_groups(w_v[0]).astype(bf16)
    wif = _gate_layout(w_if[0])
    wifq = wif[0:D_MODEL].astype(bf16)
    wifk = (wif[D_MODEL:2 * D_MODEL] / k_scale).astype(bf16)
    wifv = wif[2 * D_MODEL:].astype(bf16)
    bif = _gate_layout(b_if[0].reshape(1, -1))

    row = lambda a: a.reshape(1, -1).astype(f32)
    x2 = x.reshape(batch * seq, d)
    operands = (
        x2, row(norm_w[0]), _col_slabs(w_in[0]), pool_w[0].astype(bf16),
        row(pool_scale[0]), conv_w[0], row(conv_b[0]), wq, wkt, wv, wifq, wifk, wifv, bif,
        row(mh_norm_w[0]), row(m_skip[0]), _col_slabs(w_out[0]), row(final_norm_w))

    front_spec = pl.BlockSpec((tm, d), lambda i: (jnp.minimum(i, n_tiles - 1), 0))
    back_spec = pl.BlockSpec((tm, d), lambda i: (jnp.maximum(i - 1, 0), 0))
    in_specs = [front_spec] + [_const_spec(a.shape) for a in operands[1:]]

    out = pl.pallas_call(
        functools.partial(_fused_kernel, tiles_per_seq),
        grid=(n_tiles + 1,),
        in_specs=in_specs,
        out_specs=back_spec,
        out_shape=jax.ShapeDtypeStruct((batch * seq, d), f32),
        scratch_shapes=[
            pltpu.VMEM((tm + POOL_HALO, d), f32),
            pltpu.VMEM((tm + CONV_HALO, d), f32),
            pltpu.VMEM((tm, d), bf16),
            pltpu.VMEM((d, tm), bf16),
            pltpu.VMEM((tm, d), bf16),
            pltpu.VMEM((tm, d), f32),
            pltpu.VMEM((tm, d), f32),
            pltpu.VMEM((tm, d), f32),
            pltpu.VMEM((tm, d), f32),
            pltpu.VMEM((tm, d), f32),
            pltpu.VMEM((N_GROUPS, GROUP_DIM, GROUP_DIM), f32),
            pltpu.VMEM((N_GROUPS, GROUP_DIM, CHUNK), f32),
            pltpu.VMEM((SUBLANES, LANES), f32),
            pltpu.VMEM((SUBLANES, LANES), f32),
            pltpu.VMEM((d, GATE_COLS), bf16),
            pltpu.VMEM((d, GATE_COLS), bf16),
        ],
        compiler_params=pltpu.CompilerParams(
            dimension_semantics=("arbitrary",),
            vmem_limit_bytes=VMEM_LIMIT_BYTES),
        name="hybrid_pool_mlstm_block",
    )(*operands)
    return out.reshape(batch, seq, d)
```

```python
import functools

import jax
import jax.numpy as jnp
from jax import lax
from jax.experimental import pallas as pl
from jax.experimental.pallas import tpu as pltpu

D_MODEL = 1024
POOL_WINDOWS = (2, 4, 8, 16)
N_GROUPS = 4
GROUP_DIM = 256
QKV_BLOCK = 4
CONV_K = 4
CHUNK = 128
EPS = 1e-6

TILE_TOKENS = 256
POOL_HALO = 16
CONV_HALO = 8
SUBLANES = 8
LANES = 128
GATE_COLS = 2 * LANES
F_COPY_LANE = 8
SLAB_COLS = 256
SLABS_PER_D = D_MODEL // SLAB_COLS
VMEM_LIMIT_BYTES = 58 * 1024 * 1024

_NT = (((1,), (1,)), ((), ()))
_TN = (((0,), (0,)), ((), ()))


def _dot(a, b):
    return jnp.dot(a, b, preferred_element_type=jnp.float32)


def _dot_slabs(a, w_ref, first_slab, k_lo, k_hi):
    return jnp.concatenate(
        [_dot(a, w_ref[first_slab + s, k_lo:k_hi, :]) for s in range(SLABS_PER_D)], axis=1)


def _sublane_scan(pieces, op, identity, sub):
    out, carry = [], None
    for v in pieces:
        for s in (1, 2, 4):
            v = op(v, jnp.where(sub >= s, pltpu.roll(v, s, axis=0), identity))
        if carry is not None:
            v = op(v, carry)
        carry = jnp.broadcast_to(v[SUBLANES - 1:SUBLANES, :], v.shape)
        out.append(v)
    return out


def _fused_kernel(tiles_per_seq,
                  x_ref, normw_ref, win_ref, poolw_ref, pscale_ref, convw_ref,
                  convb_ref, wq_ref, wkt_ref, wv_ref, wifq_ref, wifk_ref, wifv_ref, bif_ref,
                  mhw_ref, mskip_ref, wout_ref, fnw_ref,
                  o_ref,
                  px_ref, mxe_ref, q_ref, kt_ref, v_ref,
                  xc_ref, mz_ref, mo_ref, hs_ref, resid_ref,
                  ct_ref, n_ref, mcol_ref, mrow_ref, ax_ref, am_ref):
    tm = TILE_TOKENS
    f32, bf16 = jnp.float32, jnp.bfloat16
    i = pl.program_id(0)
    j = lax.rem(i, tiles_per_seq)

    @pl.when(i == 0)
    def _():
        for g in range(N_GROUPS):
            r = slice(g * GROUP_DIM, (g + 1) * GROUP_DIM)
            ax = _dot(wq_ref[g], wifq_ref[r, :]) + lax.dot_general(
                wkt_ref[g], wifk_ref[r, :], _TN, preferred_element_type=f32)
            ax_ref[r, :] = ax.astype(bf16)
            am_ref[r, :] = _dot(wv_ref[g], wifv_ref[r, :]).astype(bf16)
        xc_ref[...] = jnp.zeros(xc_ref.shape, f32)
        mz_ref[...] = jnp.zeros(mz_ref.shape, f32)
        mo_ref[...] = jnp.zeros(mo_ref.shape, f32)
        hs_ref[...] = jnp.zeros(hs_ref.shape, f32)
        resid_ref[...] = jnp.zeros(resid_ref.shape, f32)

    @pl.when(j == 0)
    def _():
        px_ref[0:POOL_HALO, :] = jnp.zeros((POOL_HALO, D_MODEL), f32)
        mxe_ref[0:CONV_HALO, :] = jnp.zeros((CONV_HALO, D_MODEL), f32)
        ct_ref[...] = jnp.zeros(ct_ref.shape, f32)
        n_ref[...] = jnp.zeros(n_ref.shape, f32)
        mcol_ref[...] = jnp.zeros(mcol_ref.shape, f32)
        mrow_ref[...] = jnp.zeros(mrow_ref.shape, f32)

    resid_prev = resid_ref[...]
    y_mlstm = []
    for h in range(N_GROUPS):
        cols = slice(h * GROUP_DIM, (h + 1) * GROUP_DIM)
        hg = jax.nn.sigmoid(mo_ref[:, cols]) * hs_ref[:, cols]
        mu = jnp.mean(hg, axis=-1, keepdims=True)
        cen = hg - mu
        var = jnp.mean(cen * cen, axis=-1, keepdims=True)
        hn = cen * lax.rsqrt(var + EPS) * mhw_ref[:, cols]
        zg = mz_ref[:, cols]
        out = (hn + mskip_ref[:, cols] * xc_ref[:, cols]) * (zg * jax.nn.sigmoid(zg))
        y_mlstm.append(out.astype(bf16))
    y_mlstm = jnp.concatenate(y_mlstm, axis=1)

    x = x_ref[...]
    u = x * lax.rsqrt(jnp.mean(x * x, axis=-1, keepdims=True) + EPS)
    u = (u * normw_ref[...]).astype(bf16)

    def proj(k):
        return _dot_slabs(u, win_ref, k * SLABS_PER_D, 0, D_MODEL)

    mxe_ref[CONV_HALO:CONV_HALO + tm, :] = proj(2)
    me = mxe_ref[...]
    conv = convb_ref[...] + convw_ref[CONV_K - 1:CONV_K, :] * me[CONV_HALO:, :]
    for back in range(1, CONV_K):
        tap = CONV_K - 1 - back
        conv = conv + convw_ref[tap:tap + 1, :] * pltpu.roll(me, back, axis=0)[CONV_HALO:, :]
    xc = conv * jax.nn.sigmoid(conv)
    xc_ref[...] = xc
    mxe_ref[0:CONV_HALO, :] = mxe_ref[tm:tm + CONV_HALO, :]
    xcb = xc.astype(bf16)
    mxb = me[CONV_HALO:, :].astype(bf16)

    gates = _dot(xcb, ax_ref[...]) + _dot(mxb, am_ref[...]) + bif_ref[...]
    ig_all = gates[:, :LANES]
    fp = gates[:, LANES:]
    logf_all = jnp.minimum(fp, 0.0) - jnp.log(1.0 + jnp.exp(-jnp.abs(fp)))

    sub = lax.broadcasted_iota(jnp.int32, (SUBLANES, LANES), 0)
    tri = (lax.broadcasted_iota(jnp.int32, (CHUNK, CHUNK), 0)
           >= lax.broadcasted_iota(jnp.int32, (CHUNK, CHUNK), 1))

    scans = []
    for c in range(tm // CHUNK):
        groups = [slice(c * CHUNK + SUBLANES * r, c * CHUNK + SUBLANES * (r + 1))
                  for r in range(CHUNK // SUBLANES)]
        b_p = _sublane_scan([logf_all[r, :] for r in groups], jnp.add, 0.0, sub)
        ub_p = [ig_all[r, :] - bp for r, bp in zip(groups, b_p)]
        cmax_p = _sublane_scan(ub_p, jnp.maximum, -jnp.inf, sub)
        b = jnp.concatenate(b_p, axis=0)
        ub = jnp.concatenate(ub_p, axis=0)
        cmax = jnp.concatenate(cmax_p, axis=0)
        ub_t = ub.T
        scans.append((b, cmax, ub_t[0:SUBLANES, :], ub_t[F_COPY_LANE:F_COPY_LANE + SUBLANES, :]))

    for g in range(N_GROUPS):
        cols = slice(g * GROUP_DIM, (g + 1) * GROUP_DIM)
        q_ref[:, cols] = _dot(xcb[:, cols], wq_ref[g]).astype(bf16)
        kt_ref[cols, :] = lax.dot_general(wkt_ref[g], xcb[:, cols], _NT,
                                          preferred_element_type=f32).astype(bf16)
        v_ref[:, cols] = _dot(mxb[:, cols], wv_ref[g]).astype(bf16)

    px_ref[POOL_HALO:POOL_HALO + tm, :] = proj(0)
    pos = j * tm + lax.broadcasted_iota(jnp.int32, (tm, LANES), 0)
    pz = proj(1)
    y_pool = []
    for g, win in enumerate(POOL_WINDOWS):
        cols = slice(g * GROUP_DIM, (g + 1) * GROUP_DIM)
        a = px_ref[:, cols]
        wsum, span = a, 1
        while span < win:
            wsum = wsum + pltpu.roll(wsum, span, axis=0)
            span *= 2
        inv = 1.0 / jnp.minimum(pos + 1, win).astype(f32)
        inv = jnp.concatenate([inv, inv], axis=1)
        d = wsum[POOL_HALO:, :] * inv - a[POOL_HALO:, :]
        y = _dot(d.astype(bf16), poolw_ref[g])
        zg = pz[:, cols]
        y = y * pscale_ref[:, cols] * (zg * jax.nn.sigmoid(zg))
        y_pool.append(y.astype(bf16))
    y_pool = jnp.concatenate(y_pool, axis=1)
    resid_ref[...] = x + _dot_slabs(y_pool, wout_ref, 0, 0, D_MODEL)
    px_ref[0:POOL_HALO, :] = px_ref[tm:tm + POOL_HALO, :]

    def recurrence(c):
        rows = slice(c * CHUNK, (c + 1) * CHUNK)
        b, cmax, ub_rows, negb_rows = scans[c]
        m_row = mrow_ref[0:1, :]
        g_row = b[CHUNK - 1:CHUNK, :]
        stab = jnp.maximum(cmax, m_row)
        inter = jnp.exp(m_row - stab)
        exp_neg_mt = jnp.exp(-(b + stab))
        m_new_row = jnp.maximum(g_row + m_row, g_row + cmax[CHUNK - 1:CHUNK, :])
        mrow_ref[...] = jnp.broadcast_to(m_new_row, mrow_ref.shape)

        m_col = mcol_ref[...]
        g_col = -negb_rows[:, CHUNK - 1:CHUNK]
        m_new_col = jnp.maximum(g_col + m_col,
                                g_col + jnp.max(ub_rows, axis=1, keepdims=True))
        decay = jnp.exp(g_col + m_col - m_new_col)
        w_state = jnp.exp(g_col + ub_rows - m_new_col)
        mcol_ref[...] = m_new_col

        for h in range(N_GROUPS):
            cols = slice(h * GROUP_DIM, (h + 1) * GROUP_DIM)
            q = q_ref[rows, cols]
            kt = kt_ref[cols, rows]
            v = v_ref[rows, cols]
            stab_c = stab[:, h:h + 1]
            inter_c = inter[:, h:h + 1]
            emt_c = exp_neg_mt[:, h:h + 1]

            n_old = n_ref[h]
            s_ext = _dot(q, jnp.concatenate([kt, n_old.astype(bf16)], axis=1))
            e = jnp.where(tri, ub_rows[h:h + 1, :] - stab_c, -jnp.inf)
            wt = jnp.exp(e) * s_ext[:, :CHUNK]
            ct = ct_ref[h]
            num = _dot(wt.astype(bf16), v) + inter_c * _dot(q, ct.astype(bf16))
            den = jnp.sum(wt, axis=-1, keepdims=True) + inter_c * s_ext[:, CHUNK:CHUNK + 1]
            hs_ref[rows, cols] = num * (1.0 / jnp.maximum(jnp.abs(den), emt_c))

            ktw = kt.astype(f32) * w_state[h:h + 1, :]
            dec = decay[h:h + 1, 0:1]
            ct_ref[h] = dec * ct + _dot(ktw.astype(bf16), v)
            n_ref[h] = dec * n_old + jnp.sum(ktw, axis=1, keepdims=True)

    recurrence(0)
    mz_ref[...] = proj(3)
    mo_ref[...] = proj(4)
    for c in range(1, tm // CHUNK):
        recurrence(c)

    y = resid_prev + _dot_slabs(y_mlstm, wout_ref, 0, D_MODEL, 2 * D_MODEL)
    y = y * lax.rsqrt(jnp.mean(y * y, axis=-1, keepdims=True) + EPS)
    o_ref[...] = y * fnw_ref[...]


def _block_diag_groups(w, transpose=False):
    if transpose:
        w = jnp.swapaxes(w, 1, 2)
    w2 = w.reshape(D_MODEL, QKV_BLOCK)
    tiled = jnp.tile(w2, (1, GROUP_DIM // QKV_BLOCK))
    r = lax.broadcasted_iota(jnp.int32, (D_MODEL, GROUP_DIM), 0)
    s = lax.broadcasted_iota(jnp.int32, (D_MODEL, GROUP_DIM), 1)
    same_block = (r % GROUP_DIM) // QKV_BLOCK == s // QKV_BLOCK
    return jnp.where(same_block, tiled, 0.0).reshape(N_GROUPS, GROUP_DIM, GROUP_DIM)


def _slab_cast_kernel(w_ref, o_ref):
    o_ref[0] = w_ref[...].astype(jnp.bfloat16)


def _col_slabs(w):
    k, n = w.shape
    return pl.pallas_call(
        _slab_cast_kernel,
        grid=(n // SLAB_COLS,),
        in_specs=[pl.BlockSpec((k, SLAB_COLS), lambda s: (0, s))],
        out_specs=pl.BlockSpec((1, k, SLAB_COLS), lambda s: (s, 0, 0)),
        out_shape=jax.ShapeDtypeStruct((n // SLAB_COLS, k, SLAB_COLS), jnp.bfloat16),
        name="weight_slabs",
    )(w)


def _const_spec(shape):
    zeros = (0,) * len(shape)
    return pl.BlockSpec(shape, lambda i: zeros, pipeline_mode=pl.Buffered(1))


def _gate_layout(w):
    rows = w.shape[0]
    wi, wf = w[:, :N_GROUPS], w[:, N_GROUPS:]
    z = lambda n: jnp.zeros((rows, n), w.dtype)
    return jnp.concatenate(
        [wi, z(LANES - N_GROUPS),
         wf, z(F_COPY_LANE - N_GROUPS), wf, z(LANES - F_COPY_LANE - N_GROUPS)], axis=1)


@jax.jit
def kernel(x, norm_w, w_in, pool_w, pool_scale, conv_w, conv_b, w_q, w_k, w_v, w_if, b_if,
           mh_norm_w, m_skip, w_out, final_norm_w):
    batch, seq, d = x.shape
    assert d == D_MODEL and norm_w.shape[0] == 1 and seq % TILE_TOKENS == 0
    tm = TILE_TOKENS
    tiles_per_seq = seq // tm
    n_tiles = batch * tiles_per_seq
    bf16, f32 = jnp.bfloat16, jnp.float32

    k_scale = GROUP_DIM ** -0.5
    wq = _block_diag_groups(w_q[0]).astype(bf16)
    wkt = (_block_diag_groups(w_k[0], transpose=True) * k_scale).astype(bf16)
    wv = _block_diag_groups(w_v[0]).astype(bf16)
    wif = _gate_layout(w_if[0])
    wifq = wif[0:D_MODEL].astype(bf16)
    wifk = (wif[D_MODEL:2 * D_MODEL] / k_scale).astype(bf16)
    wifv = wif[2 * D_MODEL:].astype(bf16)
    bif = _gate_layout(b_if[0].reshape(1, -1))

    row = lambda a: a.reshape(1, -1).astype(f32)
    x2 = x.reshape(batch * seq, d)
    operands = (
        x2, row(norm_w[0]), _col_slabs(w_in[0]), pool_w[0].astype(bf16),
        row(pool_scale[0]), conv_w[0], row(conv_b[0]), wq, wkt, wv, wifq, wifk, wifv, bif,
        row(mh_norm_w[0]), row(m_skip[0]), _col_slabs(w_out[0]), row(final_norm_w))

    front_spec = pl.BlockSpec((tm, d), lambda i: (jnp.minimum(i, n_tiles - 1), 0))
    back_spec = pl.BlockSpec((tm, d), lambda i: (jnp.maximum(i - 1, 0), 0))
    in_specs = [front_spec] + [_const_spec(a.shape) for a in operands[1:]]

    out = pl.pallas_call(
        functools.partial(_fused_kernel, tiles_per_seq),
        grid=(n_tiles + 1,),
        in_specs=in_specs,
        out_specs=back_spec,
        out_shape=jax.ShapeDtypeStruct((batch * seq, d), f32),
        scratch_shapes=[
            pltpu.VMEM((tm + POOL_HALO, d), f32),
            pltpu.VMEM((tm + CONV_HALO, d), f32),
            pltpu.VMEM((tm, d), bf16),
            pltpu.VMEM((d, tm), bf16),
            pltpu.VMEM((tm, d), bf16),
            pltpu.VMEM((tm, d), f32),
            pltpu.VMEM((tm, d), f32),
            pltpu.VMEM((tm, d), f32),
            pltpu.VMEM((tm, d), f32),
            pltpu.VMEM((tm, d), f32),
            pltpu.VMEM((N_GROUPS, GROUP_DIM, GROUP_DIM), f32),
            pltpu.VMEM((N_GROUPS, GROUP_DIM, CHUNK), f32),
            pltpu.VMEM((SUBLANES, LANES), f32),
            pltpu.VMEM((SUBLANES, LANES), f32),
            pltpu.VMEM((d, GATE_COLS), bf16),
            pltpu.VMEM((d, GATE_COLS), bf16),
        ],
        compiler_params=pltpu.CompilerParams(
            dimension_semantics=("arbitrary",),
            vmem_limit_bytes=VMEM_LIMIT_BYTES),
        name="hybrid_pool_mlstm_block",
    )(*operands)
    return out.reshape(batch, seq, d)
```

```python
import functools

import jax
import jax.numpy as jnp
from jax import lax
from jax.experimental import pallas as pl
from jax.experimental.pallas import tpu as pltpu

D_MODEL = 1024
POOL_WINDOWS = (2, 4, 8, 16)
N_GROUPS = 4
GROUP_DIM = 256
QKV_BLOCK = 4
CONV_K = 4
CHUNK = 128
EPS = 1e-6

TILE_TOKENS = 256
POOL_HALO = 16
CONV_HALO = 8
SUBLANES = 8
LANES = 128
GATE_COLS = 2 * LANES
F_COPY_LANE = 8
SLAB_COLS = 256
SLABS_PER_D = D_MODEL // SLAB_COLS
WEIGHT_STAGE_SLOTS = 4
VMEM_LIMIT_BYTES = 58 * 1024 * 1024

_NT = (((1,), (1,)), ((), ()))
_TN = (((0,), (0,)), ((), ()))


def _dot(a, b):
    return jnp.dot(a, b, preferred_element_type=jnp.float32)


def _dot_slabs(a, w_ref, first_slab, k_lo, k_hi):
    return jnp.concatenate(
        [_dot(a, w_ref[first_slab + s, k_lo:k_hi, :]) for s in range(SLABS_PER_D)], axis=1)


def _sublane_scan(pieces, op, identity, sub):
    out, carry = [], None
    for v in pieces:
        for s in (1, 2, 4):
            v = op(v, jnp.where(sub >= s, pltpu.roll(v, s, axis=0), identity))
        if carry is not None:
            v = op(v, carry)
        carry = jnp.broadcast_to(v[SUBLANES - 1:SUBLANES, :], v.shape)
        out.append(v)
    return out


def _fused_kernel(tiles_per_seq,
                  x_ref, xprev_ref, normw_ref, win_hbm, poolw_in_ref, pscale_ref, convw_ref,
                  convb_ref, wqkv_in_ref, wif_in_ref, bif_in_ref,
                  mhw_ref, mskip_ref, wout_hbm, fnw_ref,
                  o_ref,
                  px_ref, mxe_ref, q_ref, kt_ref, v_ref,
                  xc_ref, mz_ref, mo_ref, hs_ref, ypool_ref,
                  ct_ref, n_ref, mcol_ref, mrow_ref, ax_ref, am_ref,
                  wq_ref, wkt_ref, wv_ref, bif_ref, win_ref, wout_ref, stage_ref, stage_sem):
    tm = TILE_TOKENS
    f32, bf16 = jnp.float32, jnp.bfloat16
    i = pl.program_id(0)
    j = lax.rem(i, tiles_per_seq)

    @pl.when(i == 0)
    def _():
        n_in = win_ref.shape[0]
        k_halves = wout_ref.shape[1] // D_MODEL
        n_pieces = n_in + wout_ref.shape[0] * k_halves
        n_slots = stage_ref.shape[0]

        def piece_copy(t):
            if t < n_in:
                src = win_hbm.at[:, pl.ds(t * SLAB_COLS, SLAB_COLS)]
            else:
                s, half = divmod(t - n_in, k_halves)
                src = wout_hbm.at[pl.ds(half * D_MODEL, D_MODEL), pl.ds(s * SLAB_COLS, SLAB_COLS)]
            slot = t % n_slots
            return pltpu.make_async_copy(src, stage_ref.at[slot], stage_sem.at[slot])

        for t in range(n_slots - 1):
            piece_copy(t).start()
        for t in range(n_pieces):
            if t + n_slots - 1 < n_pieces:
                piece_copy(t + n_slots - 1).start()
            piece_copy(t).wait()
            piece = stage_ref[t % n_slots].astype(bf16)
            if t < n_in:
                win_ref[t] = piece
            else:
                s, half = divmod(t - n_in, k_halves)
                wout_ref[s, half * D_MODEL:(half + 1) * D_MODEL, :] = piece

        row = lax.broadcasted_iota(jnp.int32, (GROUP_DIM, GATE_COLS), 0)
        lane = lax.broadcasted_iota(jnp.int32, (GROUP_DIM, GATE_COLS), 1)
        same_block = lax.shift_right_logical(row, 2) == lax.shift_right_logical(lane, 2)

        def block_diag(w_rows):
            out = jnp.zeros((GROUP_DIM, GROUP_DIM), f32)
            for b in range(QKV_BLOCK):
                out = jnp.where(same_block & ((lane & (QKV_BLOCK - 1)) == b), w_rows[:, b:b + 1], out)
            return out

        def gate_lanes(w_rows):
            lanes = lane[0:w_rows.shape[0], :]
            out = jnp.zeros(lanes.shape, f32)
            for h in range(N_GROUPS):
                out = jnp.where(lanes == h, w_rows[:, h:h + 1], out)
                f_col = w_rows[:, N_GROUPS + h:N_GROUPS + h + 1]
                out = jnp.where((lanes == LANES + h) | (lanes == LANES + F_COPY_LANE + h), f_col, out)
            return out

        k_scale = GROUP_DIM ** -0.5
        for g in range(N_GROUPS):
            r = slice(g * GROUP_DIM, (g + 1) * GROUP_DIM)
            rk = slice(D_MODEL + g * GROUP_DIM, D_MODEL + (g + 1) * GROUP_DIM)
            rv = slice(2 * D_MODEL + g * GROUP_DIM, 2 * D_MODEL + (g + 1) * GROUP_DIM)
            wq_g = block_diag(wqkv_in_ref[r, :]).astype(bf16)
            wkt_g = (block_diag(wqkv_in_ref[rk, :]) * k_scale).astype(bf16)
            wv_g = block_diag(wqkv_in_ref[rv, :]).astype(bf16)
            wq_ref[g] = wq_g
            wkt_ref[g] = wkt_g
            wv_ref[g] = wv_g
            win_ref[g] = _dot(win_ref[g], poolw_in_ref[g].astype(bf16)).astype(bf16)
            wifq_g = gate_lanes(wif_in_ref[r, :]).astype(bf16)
            wifk_g = (gate_lanes(wif_in_ref[D_MODEL + g * GROUP_DIM:D_MODEL + (g + 1) * GROUP_DIM, :])
                      * (1.0 / k_scale)).astype(bf16)
            wifv_g = gate_lanes(
                wif_in_ref[2 * D_MODEL + g * GROUP_DIM:2 * D_MODEL + (g + 1) * GROUP_DIM, :]).astype(bf16)
            ax = _dot(wq_g, wifq_g) + lax.dot_general(wkt_g, wifk_g, _TN, preferred_element_type=f32)
            ax_ref[r, :] = ax.astype(bf16)
            am_ref[r, :] = _dot(wv_g, wifv_g).astype(bf16)
        bif_ref[...] = jnp.broadcast_to(gate_lanes(bif_in_ref[...]), bif_ref.shape)
        xc_ref[...] = jnp.zeros(xc_ref.shape, f32)
        mz_ref[...] = jnp.zeros(mz_ref.shape, f32)
        mo_ref[...] = jnp.zeros(mo_ref.shape, f32)
        hs_ref[...] = jnp.zeros(hs_ref.shape, f32)
        ypool_ref[...] = jnp.zeros(ypool_ref.shape, bf16)

    @pl.when(j == 0)
    def _():
        px_ref[0:POOL_HALO, :] = jnp.zeros((POOL_HALO, D_MODEL), f32)
        mxe_ref[0:CONV_HALO, :] = jnp.zeros((CONV_HALO, D_MODEL), f32)
        ct_ref[...] = jnp.zeros(ct_ref.shape, f32)
        n_ref[...] = jnp.zeros(n_ref.shape, f32)
        mcol_ref[...] = jnp.zeros(mcol_ref.shape, f32)
        mrow_ref[...] = jnp.zeros(mrow_ref.shape, f32)

    out_pool_prev = _dot_slabs(ypool_ref[...], wout_ref, 0, 0, D_MODEL)
    x = x_ref[...]
    u = x * lax.rsqrt(jnp.mean(x * x, axis=-1, keepdims=True) + EPS)
    u = (u * normw_ref[...]).astype(bf16)

    def slab(k, s):
        return _dot(u, win_ref[k * SLABS_PER_D + s])

    group_cols = [slice(g * GROUP_DIM, (g + 1) * GROUP_DIM) for g in range(N_GROUPS)]
    for h, cols in enumerate(group_cols):
        mxe_ref[CONV_HALO:CONV_HALO + tm, cols] = slab(2, h)

    y_mlstm = []
    for h, cols in enumerate(group_cols):
        hg = jax.nn.sigmoid(mo_ref[:, cols]) * hs_ref[:, cols]
        mu = jnp.mean(hg, axis=-1, keepdims=True)
        cen = hg - mu
        var = jnp.mean(cen * cen, axis=-1, keepdims=True)
        hn = cen * lax.rsqrt(var + EPS) * mhw_ref[:, cols]
        zg = mz_ref[:, cols]
        out = (hn + mskip_ref[:, cols] * xc_ref[:, cols]) * (zg * jax.nn.sigmoid(zg))
        y_mlstm.append(out.astype(bf16))
    y_mlstm = jnp.concatenate(y_mlstm, axis=1)

    for c, cols in enumerate(group_cols):
        px_ref[POOL_HALO:POOL_HALO + tm, cols] = slab(0, c)
    xcb, mxb = [], []
    for c, cols in enumerate(group_cols):
        me = mxe_ref[:, cols]
        conv = convb_ref[:, cols] + convw_ref[CONV_K - 1:CONV_K, cols] * me[CONV_HALO:, :]
        for back in range(1, CONV_K):
            tap = CONV_K - 1 - back
            conv = conv + convw_ref[tap:tap + 1, cols] * pltpu.roll(me, back, axis=0)[CONV_HALO:, :]
        xc = conv * jax.nn.sigmoid(conv)
        xc_ref[:, cols] = xc
        xcb.append(xc.astype(bf16))
        mxb.append(me[CONV_HALO:, :].astype(bf16))
    mxe_ref[0:CONV_HALO, :] = mxe_ref[tm:tm + CONV_HALO, :]
    xcb = jnp.concatenate(xcb, axis=1)
    mxb = jnp.concatenate(mxb, axis=1)

    pz = [slab(1, s) for s in range(SLABS_PER_D)]
    gates = _dot(jnp.concatenate([xcb, mxb], axis=1),
                 jnp.concatenate([ax_ref[...], am_ref[...]], axis=0)) + bif_ref[0:1, :]
    ig_all = gates[:, :LANES]
    fp = gates[:, LANES:]
    logf_all = jnp.minimum(fp, 0.0) - jnp.log(1.0 + jnp.exp(-jnp.abs(fp)))

    sub = lax.broadcasted_iota(jnp.int32, (SUBLANES, LANES), 0)
    tri = (lax.broadcasted_iota(jnp.int32, (CHUNK, CHUNK), 0)
           >= lax.broadcasted_iota(jnp.int32, (CHUNK, CHUNK), 1))

    scans = []
    for c in range(tm // CHUNK):
        groups = [slice(c * CHUNK + SUBLANES * r, c * CHUNK + SUBLANES * (r + 1))
                  for r in range(CHUNK // SUBLANES)]
        b_p = _sublane_scan([logf_all[r, :] for r in groups], jnp.add, 0.0, sub)
        ub_p = [ig_all[r, :] - bp for r, bp in zip(groups, b_p)]
        cmax_p = _sublane_scan(ub_p, jnp.maximum, -jnp.inf, sub)
        b = jnp.concatenate(b_p, axis=0)
        ub = jnp.concatenate(ub_p, axis=0)
        cmax = jnp.concatenate(cmax_p, axis=0)
        ub_t = ub.T
        scans.append((b, cmax, ub_t[0:SUBLANES, :], ub_t[F_COPY_LANE:F_COPY_LANE + SUBLANES, :]))

    for g in range(N_GROUPS):
        cols = slice(g * GROUP_DIM, (g + 1) * GROUP_DIM)
        q_ref[:, cols] = _dot(xcb[:, cols], wq_ref[g]).astype(bf16)
        kt_ref[cols, :] = lax.dot_general(wkt_ref[g], xcb[:, cols], _NT,
                                          preferred_element_type=f32).astype(bf16)
        v_ref[:, cols] = _dot(mxb[:, cols], wv_ref[g]).astype(bf16)

    pos = j * tm + lax.broadcasted_iota(jnp.int32, (tm, LANES), 0)
    y_pool = []
    for g, win in enumerate(POOL_WINDOWS):
        cols = group_cols[g]
        a = px_ref[:, cols]
        wsum, span = a, 1
        while span < win:
            wsum = wsum + pltpu.roll(wsum, span, axis=0)
            span *= 2
        inv = 1.0 / jnp.minimum(pos + 1, win).astype(f32)
        inv = jnp.concatenate([inv, inv], axis=1)
        y = wsum[POOL_HALO:, :] * inv - a[POOL_HALO:, :]
        zg = pz[g]
        y = y * pscale_ref[:, cols] * (zg * jax.nn.sigmoid(zg))
        y_pool.append(y.astype(bf16))
    y_pool = jnp.concatenate(y_pool, axis=1)
    ypool_ref[...] = y_pool
    px_ref[0:POOL_HALO, :] = px_ref[tm:tm + POOL_HALO, :]

    heads = range(N_GROUPS)

    def rec_scores(c):
        rows = slice(c * CHUNK, (c + 1) * CHUNK)
        q = [q_ref[rows, group_cols[h]] for h in heads]
        kt = [kt_ref[group_cols[h], rows] for h in heads]
        v = [v_ref[rows, group_cols[h]] for h in heads]
        n_old = [n_ref[h] for h in heads]
        ct = [ct_ref[h] for h in heads]
        s_ext = [_dot(q[h], jnp.concatenate([kt[h], n_old[h].astype(bf16)], axis=1))
                 for h in heads]
        q_ct = [_dot(q[h], ct[h].astype(bf16)) for h in heads]
        return rows, kt, v, n_old, ct, s_ext, q_ct

    def rec_finish(c, scores):
        rows, kt, v, n_old, ct, s_ext, q_ct = scores
        b, cmax, ub_rows, negb_rows = scans[c]
        m_row = mrow_ref[0:1, :]
        g_row = b[CHUNK - 1:CHUNK, :]
        stab = jnp.maximum(cmax, m_row)
        inter = jnp.exp(m_row - stab)
        exp_neg_mt = jnp.exp(-(b + stab))
        m_new_row = jnp.maximum(g_row + m_row, g_row + cmax[CHUNK - 1:CHUNK, :])
        mrow_ref[...] = jnp.broadcast_to(m_new_row, mrow_ref.shape)

        m_col = mcol_ref[...]
        g_col = -negb_rows[:, CHUNK - 1:CHUNK]
        m_new_col = jnp.maximum(g_col + m_col,
                                g_col + jnp.max(ub_rows, axis=1, keepdims=True))
        decay = jnp.exp(g_col + m_col - m_new_col)
        w_state = jnp.exp(g_col + ub_rows - m_new_col)
        mcol_ref[...] = m_new_col

        wt = [jnp.exp(jnp.where(tri, ub_rows[h:h + 1, :] - stab[:, h:h + 1], -jnp.inf))
              * s_ext[h][:, :CHUNK] for h in heads]
        ktw = [kt[h].astype(f32) * w_state[h:h + 1, :] for h in heads]
        wv = [_dot(jnp.concatenate([wt[h].astype(bf16), ktw[h].astype(bf16)], axis=0), v[h])
              for h in heads]
        for h in heads:
            inter_c = inter[:, h:h + 1]
            num = wv[h][:CHUNK, :] + inter_c * q_ct[h]
            den = (jnp.sum(wt[h], axis=-1, keepdims=True)
                   + inter_c * s_ext[h][:, CHUNK:CHUNK + 1])
            hs_ref[rows, group_cols[h]] = num * (
                1.0 / jnp.maximum(jnp.abs(den), exp_neg_mt[:, h:h + 1]))
        for h in heads:
            dec = decay[h:h + 1, 0:1]
            ct_ref[h] = dec * ct[h] + wv[h][CHUNK:, :]
            n_ref[h] = dec * n_old[h] + jnp.sum(ktw[h], axis=1, keepdims=True)

    assert tm // CHUNK == 2
    out_mlstm_prev = _dot_slabs(y_mlstm, wout_ref, 0, D_MODEL, 2 * D_MODEL)
    scores0 = rec_scores(0)
    for g in (0, 1):
        mz_ref[:, group_cols[g]] = slab(3, g)

    y = xprev_ref[...] + out_pool_prev + out_mlstm_prev
    y = y * lax.rsqrt(jnp.mean(y * y, axis=-1, keepdims=True) + EPS)
    o_ref[...] = y * fnw_ref[...]

    rec_finish(0, scores0)
    for g in (2, 3):
        mz_ref[:, group_cols[g]] = slab(3, g)
    scores1 = rec_scores(1)
    for g in (0, 1):
        mo_ref[:, group_cols[g]] = slab(4, g)
    rec_finish(1, scores1)
    for g in (2, 3):
        mo_ref[:, group_cols[g]] = slab(4, g)


def _const_spec(shape):
    zeros = (0,) * len(shape)
    return pl.BlockSpec(shape, lambda i: zeros, pipeline_mode=pl.Buffered(1))


@jax.jit
def kernel(x, norm_w, w_in, pool_w, pool_scale, conv_w, conv_b, w_q, w_k, w_v, w_if, b_if,
           mh_norm_w, m_skip, w_out, final_norm_w):
    batch, seq, d = x.shape
    assert d == D_MODEL and norm_w.shape[0] == 1 and seq % TILE_TOKENS == 0
    tm = TILE_TOKENS
    tiles_per_seq = seq // tm
    n_tiles = batch * tiles_per_seq
    bf16, f32 = jnp.bfloat16, jnp.float32

    wqkv2 = jnp.concatenate([w_q[0], jnp.swapaxes(w_k[0], 1, 2), w_v[0]], axis=0).reshape(
        3 * D_MODEL, QKV_BLOCK)

    row = lambda a: a.reshape(1, -1)
    x2 = x.reshape(batch * seq, d)
    operands = (
        x2, x2, row(norm_w[0]), w_in[0], pool_w[0],
        row(pool_scale[0]), conv_w[0], row(conv_b[0]), wqkv2, w_if[0], row(b_if[0]),
        row(mh_norm_w[0]), row(m_skip[0]), w_out[0], row(final_norm_w))
    hbm_operands = (3, 13)

    front_spec = pl.BlockSpec((tm, d), lambda i: (jnp.minimum(i, n_tiles - 1), 0))
    back_spec = pl.BlockSpec((tm, d), lambda i: (jnp.maximum(i - 1, 0), 0))
    in_specs = [front_spec, back_spec] + [
        pl.BlockSpec(memory_space=pl.ANY) if k in hbm_operands else _const_spec(a.shape)
        for k, a in enumerate(operands) if k >= 2]

    out = pl.pallas_call(
        functools.partial(_fused_kernel, tiles_per_seq),
        grid=(n_tiles + 1,),
        in_specs=in_specs,
        out_specs=back_spec,
        out_shape=jax.ShapeDtypeStruct((batch * seq, d), f32),
        scratch_shapes=[
            pltpu.VMEM((tm + POOL_HALO, d), f32),
            pltpu.VMEM((tm + CONV_HALO, d), f32),
            pltpu.VMEM((tm, d), bf16),
            pltpu.VMEM((d, tm), bf16),
            pltpu.VMEM((tm, d), bf16),
            pltpu.VMEM((tm, d), f32),
            pltpu.VMEM((tm, d), f32),
            pltpu.VMEM((tm, d), f32),
            pltpu.VMEM((tm, d), f32),
            pltpu.VMEM((tm, d), bf16),
            pltpu.VMEM((N_GROUPS, GROUP_DIM, GROUP_DIM), f32),
            pltpu.VMEM((N_GROUPS, GROUP_DIM, CHUNK), f32),
            pltpu.VMEM((SUBLANES, LANES), f32),
            pltpu.VMEM((SUBLANES, LANES), f32),
            pltpu.VMEM((d, GATE_COLS), bf16),
            pltpu.VMEM((d, GATE_COLS), bf16),
            pltpu.VMEM((N_GROUPS, GROUP_DIM, GROUP_DIM), bf16),
            pltpu.VMEM((N_GROUPS, GROUP_DIM, GROUP_DIM), bf16),
            pltpu.VMEM((N_GROUPS, GROUP_DIM, GROUP_DIM), bf16),
            pltpu.VMEM((SUBLANES, GATE_COLS), f32),
            pltpu.VMEM((w_in.shape[2] // SLAB_COLS, d, SLAB_COLS), bf16),
            pltpu.VMEM((w_out.shape[2] // SLAB_COLS, 2 * d, SLAB_COLS), bf16),
            pltpu.VMEM((WEIGHT_STAGE_SLOTS, d, SLAB_COLS), f32),
            pltpu.SemaphoreType.DMA((WEIGHT_STAGE_SLOTS,)),
        ],
        compiler_params=pltpu.CompilerParams(
            dimension_semantics=("arbitrary",),
            vmem_limit_bytes=VMEM_LIMIT_BYTES),
        name="hybrid_pool_mlstm_block",
    )(*operands)
    return out.reshape(batch, seq, d)
```
